```python
import jax
import jax.numpy as jnp
from jax import lax
import numpy as np

D_MODEL = 1024
BATCH = 4
SEQ = 8192
DEPTH = 1

N_META = 16
EPS = 1e-6

POOL_WINDOWS = (2, 4, 8, 16)
POOL_WIDTH = D_MODEL
POOL_GROUP = POOL_WIDTH // len(POOL_WINDOWS)

SSD_INNER = 2 * D_MODEL
SSD_HEAD_DIM = 64
SSD_HEADS = SSD_INNER // SSD_HEAD_DIM
SSD_GROUPS = 4
SSD_HPG = SSD_HEADS // SSD_GROUPS
SSD_STATE = 128
SSD_CONV = 4
SSD_CHUNK = 128
SSD_CONV_DIM = SSD_INNER + 2 * SSD_GROUPS * SSD_STATE
SSD_LEAD_PAD = SSD_CHUNK - N_META

N_EXPERT_GROUPS = 8
EXPERTS_PER_GROUP = 8
N_EXPERTS = N_EXPERT_GROUPS * EXPERTS_PER_GROUP
TOP_K = 2
D_EXPERT = 512
MOE_BLOCK = 256

IN_WIDTHS = (POOL_WIDTH, SSD_INNER, SSD_CONV_DIM, SSD_HEADS, D_MODEL, D_MODEL)
IN_SPLITS = tuple(int(s) for s in np.cumsum(IN_WIDTHS[:-1]))
D_IN_PROJ = sum(IN_WIDTHS)

kernel_name = 'hybrid_pool_ssd_hier_moe'


def rms_norm(x, g):
    xf = x.astype(jnp.float32)
    y = xf * lax.rsqrt(jnp.mean(xf * xf, axis=-1, keepdims=True) + EPS)
    return (y * g.astype(jnp.float32)).astype(x.dtype)


def pool_mixer(u, w_pool, pool_scale):
    bsz, L, _ = u.shape
    uf = u.astype(jnp.float32)
    s0 = jnp.concatenate([jnp.zeros_like(uf[:, :1]), jnp.cumsum(uf, axis=1)], axis=1)
    t = jnp.arange(L)
    outs = []
    for k, w in enumerate(POOL_WINDOWS):
        lo, hi = k * POOL_GROUP, (k + 1) * POOL_GROUP
        sk = s0[:, :, lo:hi]
        lag = jnp.concatenate([jnp.zeros((bsz, w - 1, POOL_GROUP), jnp.float32), sk[:, :L - w + 1]], axis=1)
        cnt = jnp.minimum(t + 1, w).astype(jnp.float32)[None, :, None]
        d = ((sk[:, 1:] - lag) / cnt - uf[:, :, lo:hi]).astype(u.dtype)
        outs.append(d @ w_pool[k])
    return jnp.concatenate(outs, axis=-1) * pool_scale


def causal_dwconv(x, w, b):
    K = w.shape[0]
    L = x.shape[1]
    xp = jnp.pad(x, ((0, 0), (K - 1, 0), (0, 0)))
    y = xp[:, 0:L] * w[0]
    for k in range(1, K):
        y = y + xp[:, k:k + L] * w[k]
    return y + b


def ssd_chunked(xh, dt, A, bm, cm):
    bsz, lp = xh.shape[:2]
    nc = lp // SSD_CHUNK
    X = (xh * dt[..., None]).reshape(bsz, nc, SSD_CHUNK, SSD_GROUPS, SSD_HPG, SSD_HEAD_DIM)
    adt = (dt * A).reshape(bsz, nc, SSD_CHUNK, SSD_GROUPS, SSD_HPG)
    bc = bm.reshape(bsz, nc, SSD_CHUNK, SSD_GROUPS, SSD_STATE)
    cc = cm.reshape(bsz, nc, SSD_CHUNK, SSD_GROUPS, SSD_STATE)
    a_cs = jnp.cumsum(adt, axis=2)
    causal = jnp.tril(jnp.ones((SSD_CHUNK, SSD_CHUNK), dtype=bool))[None, None, :, :, None, None]
    diff = a_cs[:, :, :, None] - a_cs[:, :, None, :]
    cb = jnp.einsum('bclgn,bcsgn->bclsg', cc, bc)
    scores = cb[..., None] * jnp.exp(jnp.where(causal, diff, -jnp.inf))
    y_diag = jnp.einsum('bclsgr,bcsgrp->bclgrp', scores, X)
    decay_states = jnp.exp(a_cs[:, :, -1:] - a_cs)
    states = jnp.einsum('bclgn,bclgrp->bcgrpn', bc, X * decay_states[..., None])
    chunk_decay = jnp.exp(a_cs[:, :, -1])

    def step(h, inp):
        st, dec = inp
        return h * dec[..., None, None] + st, h

    h0 = jnp.zeros_like(states[:, 0])
    _, prev = lax.scan(step, h0, (jnp.moveaxis(states, 1, 0), jnp.moveaxis(chunk_decay, 1, 0)))
    prev = jnp.moveaxis(prev, 0, 1)
    y_off = jnp.einsum('bclgn,bcgrpn->bclgrp', cc, prev) * jnp.exp(a_cs)[..., None]
    return (y_diag + y_off).reshape(bsz, lp, SSD_HEADS, SSD_HEAD_DIM)


def ssd_mixer(z, xbc, dt_raw, conv_w, conv_b, dt_bias, a_log, d_skip, ssd_norm):
    bsz, L, _ = z.shape
    xbc = jax.nn.silu(causal_dwconv(xbc, conv_w, conv_b)).astype(jnp.float32)
    xs = xbc[..., :SSD_INNER].reshape(bsz, L, SSD_HEADS, SSD_HEAD_DIM)
    bm = xbc[..., SSD_INNER:SSD_INNER + SSD_GROUPS * SSD_STATE].reshape(bsz, L, SSD_GROUPS, SSD_STATE)
    cm = xbc[..., SSD_INNER + SSD_GROUPS * SSD_STATE:].reshape(bsz, L, SSD_GROUPS, SSD_STATE)
    dt = jax.nn.softplus(dt_raw.astype(jnp.float32) + dt_bias.astype(jnp.float32))
    A = -jnp.exp(a_log.astype(jnp.float32))
    padf = lambda a: jnp.pad(a, ((0, 0), (SSD_LEAD_PAD, 0)) + ((0, 0),) * (a.ndim - 2))
    y = ssd_chunked(padf(xs), padf(dt), A, padf(bm), padf(cm))[:, SSD_LEAD_PAD:]
    y = y + d_skip.astype(jnp.float32)[:, None] * xs
    y = y.reshape(bsz, L, SSD_INNER) * jax.nn.silu(z.astype(jnp.float32))
    yg = y.reshape(bsz, L, SSD_GROUPS, SSD_INNER // SSD_GROUPS)
    yg = yg * lax.rsqrt(jnp.mean(yg * yg, axis=-1, keepdims=True) + EPS)
    return (yg.reshape(bsz, L, SSD_INNER) * ssd_norm.astype(jnp.float32)).astype(z.dtype)


def hier_moe(h, w_rg, b_rg, w_re, b_re, w_gu, w_dn):
    bsz, L, D = h.shape
    T = bsz * L
    hf = h.reshape(T, D)
    g_prob = jax.nn.softmax((hf @ w_rg).astype(jnp.float32) + b_rg.astype(jnp.float32), axis=-1)
    g_p, g_idx = lax.top_k(g_prob, 1)
    e_logits = ((hf @ w_re).astype(jnp.float32) + b_re.astype(jnp.float32)).reshape(T, N_EXPERT_GROUPS, EXPERTS_PER_GROUP)
    e_logits = jnp.take_along_axis(e_logits, g_idx[:, :, None], axis=1)[:, 0]
    e_p, e_idx = lax.top_k(jax.nn.softmax(e_logits, axis=-1), TOP_K)
    e_p = e_p / jnp.sum(e_p, axis=-1, keepdims=True)
    weights = g_p * e_p
    experts = g_idx * EXPERTS_PER_GROUP + e_idx
    A = T * TOP_K
    flat_e = experts.reshape(A).astype(jnp.int32)
    flat_w = weights.reshape(A)
    order = jnp.argsort(flat_e)
    se = flat_e[order]
    stok = (order // TOP_K).astype(jnp.int32)
    sw = flat_w[order]
    counts = jnp.bincount(flat_e, length=N_EXPERTS).astype(jnp.int32)
    starts = jnp.cumsum(counts) - counts
    pcounts = (counts + MOE_BLOCK - 1) // MOE_BLOCK * MOE_BLOCK
    pends = jnp.cumsum(pcounts)
    pstarts = pends - pcounts
    dest = pstarts[se] + (jnp.arange(A, dtype=jnp.int32) - starts[se])
    nb = -(-A // MOE_BLOCK) + N_EXPERTS
    P = nb * MOE_BLOCK
    row_tok = jnp.full((P,), T, jnp.int32).at[dest].set(stok)
    row_w = jnp.zeros((P,), jnp.float32).at[dest].set(sw)
    block_e = jnp.minimum(jnp.searchsorted(pends, jnp.arange(nb, dtype=jnp.int32) * MOE_BLOCK, side='right'), N_EXPERTS - 1)
    h_pad = jnp.concatenate([hf, jnp.zeros((1, D), hf.dtype)], axis=0)
    xb = h_pad[row_tok].reshape(nb, MOE_BLOCK, D)

    def expert_block(args):
        xblk, e = args
        gu = xblk @ w_gu[e]
        g, u = jnp.split(gu, 2, axis=-1)
        return (jax.nn.silu(g) * u) @ w_dn[e]

    yb = lax.map(expert_block, (xb, block_e)).reshape(P, D)
    y = yb * row_w[:, None].astype(yb.dtype)
    out = jnp.zeros((T + 1, D), yb.dtype).at[row_tok].add(y)[:T]
    return out.reshape(bsz, L, D)


def setup_inputs(seed: int = 0) -> dict:
    key = jax.random.key(seed)
    ks = jax.random.split(key, 24)
    f = jnp.float32
    nrm = lambda k, shape, s: jax.random.normal(k, shape, f) * s
    dt0 = jnp.exp(jax.random.uniform(ks[8], (DEPTH, SSD_HEADS), f) * (np.log(0.1) - np.log(0.001)) + np.log(0.001))
    return {
        'x': nrm(ks[0], (BATCH, SEQ, D_MODEL), 1.0),
        'meta_tokens': nrm(ks[1], (N_META, D_MODEL), 1.0),
        'norm_mix': 1.0 + nrm(ks[2], (DEPTH, D_MODEL), 0.05),
        'w_in': nrm(ks[3], (DEPTH, D_MODEL, D_IN_PROJ), D_MODEL ** -0.5),
        'pool_w': nrm(ks[4], (DEPTH, len(POOL_WINDOWS), POOL_GROUP, POOL_GROUP), POOL_GROUP ** -0.5),
        'pool_scale': 1.0 + nrm(ks[5], (DEPTH, POOL_WIDTH), 0.05),
        'conv_w': nrm(ks[6], (DEPTH, SSD_CONV, SSD_CONV_DIM), SSD_CONV ** -0.5),
        'conv_b': nrm(ks[7], (DEPTH, SSD_CONV_DIM), 0.02),
        'dt_bias': dt0 + jnp.log(-jnp.expm1(-dt0)),
        'a_log': jnp.log(jax.random.uniform(ks[9], (DEPTH, SSD_HEADS), f, 1.0, 16.0)),
        'd_skip': 1.0 + nrm(ks[10], (DEPTH, SSD_HEADS), 0.1),
        'ssd_norm': 1.0 + nrm(ks[11], (DEPTH, SSD_INNER), 0.05),
        'w_pool_out': nrm(ks[12], (DEPTH, POOL_WIDTH, D_MODEL), POOL_WIDTH ** -0.5),
        'w_ssd_out': nrm(ks[13], (DEPTH, SSD_INNER, D_MODEL), SSD_INNER ** -0.5),
        'w_out': nrm(ks[14], (DEPTH, D_MODEL, D_MODEL), D_MODEL ** -0.5),
        'norm_ffn': 1.0 + nrm(ks[15], (DEPTH, D_MODEL), 0.05),
        'w_router_group': nrm(ks[16], (DEPTH, D_MODEL, N_EXPERT_GROUPS), D_MODEL ** -0.5),
        'b_router_group': nrm(ks[17], (DEPTH, N_EXPERT_GROUPS), 0.01),
        'w_router_expert': nrm(ks[18], (DEPTH, D_MODEL, N_EXPERTS), D_MODEL ** -0.5),
        'b_router_expert': nrm(ks[19], (DEPTH, N_EXPERTS), 0.01),
        'w_gate_up': nrm(ks[20], (DEPTH, N_EXPERTS, D_MODEL, 2 * D_EXPERT), D_MODEL ** -0.5),
        'w_down': nrm(ks[21], (DEPTH, N_EXPERTS, D_EXPERT, D_MODEL), D_EXPERT ** -0.5),
        'norm_final': 1.0 + nrm(ks[22], (D_MODEL,), 0.05),
    }


def reference(x, meta_tokens, norm_mix, w_in, pool_w, pool_scale, conv_w, conv_b, dt_bias, a_log, d_skip,
              ssd_norm, w_pool_out, w_ssd_out, w_out, norm_ffn, w_router_group, b_router_group,
              w_router_expert, b_router_expert, w_gate_up, w_down, norm_final):
    bsz = x.shape[0]
    meta = jnp.broadcast_to(meta_tokens.astype(x.dtype)[None], (bsz, N_META, D_MODEL))
    h = jnp.concatenate([meta, x], axis=1)
    for i in range(DEPTH):
        hn = rms_norm(h, norm_mix[i])
        proj = hn @ w_in[i]
        u, z, xbc, dt_raw, g_pool, g_ssd = jnp.split(proj, IN_SPLITS, axis=-1)
        y_pool = pool_mixer(u, pool_w[i], pool_scale[i]) @ w_pool_out[i]
        y_ssd = ssd_mixer(z, xbc, dt_raw, conv_w[i], conv_b[i], dt_bias[i], a_log[i], d_skip[i], ssd_norm[i]) @ w_ssd_out[i]
        merged = jax.nn.sigmoid(g_pool) * y_pool + jax.nn.sigmoid(g_ssd) * y_ssd
        h = h + merged @ w_out[i]
        h = h + hier_moe(rms_norm(h, norm_ffn[i]), w_router_group[i], b_router_group[i],
                         w_router_expert[i], b_router_expert[i], w_gate_up[i], w_down[i])
    return rms_norm(h, norm_final)[:, N_META:]
```

```python
import functools

import jax
import jax.numpy as jnp
from jax import lax
from jax.experimental import pallas as pl
from jax.experimental.pallas import tpu as pltpu

F32 = jnp.float32
BF16 = jnp.bfloat16
I32 = jnp.int32

D_MODEL = 1024
N_META = 16
EPS = 1e-6
CHUNK = 128
LEAD = CHUNK - N_META
CPT = 5
TM = CPT * CHUNK
POOL_WINDOWS = (2, 4, 8, 16)
POOL_GROUP = D_MODEL // len(POOL_WINDOWS)
POOL_HALO = 16
SSD_INNER = 2 * D_MODEL
SSD_HEADS = 32
SSD_HEAD_DIM = 64
SSD_GROUPS = 4
SSD_STATE = 128
SSD_CONV = 4
CONV_HALO = 8
BC_WIDTH = 2 * SSD_GROUPS * SSD_STATE
GROUP_WIDTH = SSD_INNER // SSD_GROUPS
PAIRS_PER_GROUP = GROUP_WIDTH // 128
N_EXPERT_GROUPS = 8
EXPERTS_PER_GROUP = 8
N_EXPERTS = 64
D_EXPERT = 512
MOE_BLOCK = 256
LANES = 128

COL_Z, COL_X, COL_BC, COL_U, COL_GP, COL_GS = 0, 1, 4, 5, 6, 7
PROJ_WIDTH = 8 * D_MODEL
PROJ_TN = 2048

VMEM_LIMIT = 48 * 1024 * 1024


def _cparams(*sem):
    return pltpu.CompilerParams(dimension_semantics=sem, vmem_limit_bytes=VMEM_LIMIT)


def _sigmoid(x):
    return 1.0 / (1.0 + jnp.exp(-x))


def _inproj_kernel(h_ref, g_ref, w_ref, o_ref):
    x = h_ref[...]
    hn = x * lax.rsqrt(jnp.mean(x * x, axis=-1, keepdims=True) + EPS) * g_ref[...]
    o_ref[...] = jnp.dot(hn.astype(BF16), w_ref[...], preferred_element_type=F32)


def _in_proj(hp, gain, w, tn):
    rows, n = hp.shape[0], w.shape[1]
    return pl.pallas_call(
        _inproj_kernel,
        grid=(n // tn, rows // TM),
        in_specs=[pl.BlockSpec((TM, D_MODEL), lambda j, i: (i, 0)),
                  pl.BlockSpec((1, D_MODEL), lambda j, i: (0, 0)),
                  pl.BlockSpec((D_MODEL, tn), lambda j, i: (0, j))],
        out_specs=pl.BlockSpec((TM, tn), lambda j, i: (i, j)),
        out_shape=jax.ShapeDtypeStruct((rows, n), F32),
        compiler_params=_cparams("parallel", "parallel"),
        name="in_proj",
    )(hp, gain, w)


def _pool_kernel(u_ref, uprev_ref, gp_ref, pw_ref, ps_ref, wo_ref, o_ref, ubuf, *, lp):
    i = pl.program_id(0)
    ubuf[0:POOL_HALO, :] = uprev_ref[...]
    ubuf[POOL_HALO:POOL_HALO + TM, :] = u_ref[...]
    row = lax.rem(i * TM, lp) + lax.broadcasted_iota(I32, (TM, 1), 0)
    tpos = row - LEAD
    parts = []
    for k, w in enumerate(POOL_WINDOWS):
        lo = k * POOL_GROUP
        cur = ubuf[POOL_HALO:POOL_HALO + TM, lo:lo + POOL_GROUP]
        acc = cur
        for s in range(1, w):
            acc = acc + ubuf[POOL_HALO - s:POOL_HALO - s + TM, lo:lo + POOL_GROUP]
        cnt = jnp.clip(tpos + 1, 1, w).astype(F32)
        d = acc / cnt - cur
        parts.append(jnp.dot(d.astype(BF16), pw_ref[k], preferred_element_type=F32))
    pm = jnp.concatenate(parts, axis=-1) * ps_ref[...]
    y = jnp.dot(pm.astype(BF16), wo_ref[...], preferred_element_type=F32)
    o_ref[...] = (_sigmoid(gp_ref[...]) * y).astype(BF16)


def _pool_branch(proj, pool_w, pool_scale, w_pool_out, lp):
    rows = proj.shape[0]
    halo_blocks = TM // POOL_HALO
    return pl.pallas_call(
        functools.partial(_pool_kernel, lp=lp),
        grid=(rows // TM,),
        in_specs=[pl.BlockSpec((TM, D_MODEL), lambda i: (i, COL_U)),
                  pl.BlockSpec((POOL_HALO, D_MODEL),
                               lambda i: (jnp.maximum(i * halo_blocks - 1, 0), COL_U)),
                  pl.BlockSpec((TM, D_MODEL), lambda i: (i, COL_GP)),
                  pl.BlockSpec((len(POOL_WINDOWS), POOL_GROUP, POOL_GROUP), lambda i: (0, 0, 0)),
                  pl.BlockSpec((1, D_MODEL), lambda i: (0, 0)),
                  pl.BlockSpec((D_MODEL, D_MODEL), lambda i: (0, 0))],
        out_specs=pl.BlockSpec((TM, D_MODEL), lambda i: (i, 0)),
        out_shape=jax.ShapeDtypeStruct((rows, D_MODEL), BF16),
        scratch_shapes=[pltpu.VMEM((POOL_HALO + TM, D_MODEL), F32)],
        compiler_params=_cparams("parallel"),
        name="pool_branch",
    )(proj, proj, proj, pool_w, pool_scale, w_pool_out)


def _ssd_kernel(z_ref, x_ref, xprev_ref, bc_ref, bcprev_ref, dt_ref, gs_ref,
                cw_ref, cb_ref, dtb_ref, alog_ref, dsk_ref, nrm_ref, wo_ref,
                o_ref, xbuf, act_ref, y_ref, st_ref, yn_ref):
    c = pl.program_id(1)

    @pl.when(c == 0)
    def _():
        st_ref[...] = jnp.zeros_like(st_ref)

    xbuf[0:CONV_HALO, 0:SSD_INNER] = xprev_ref[...]
    xbuf[0:CONV_HALO, SSD_INNER:] = bcprev_ref[...]
    xbuf[CONV_HALO:, 0:SSD_INNER] = x_ref[...]
    xbuf[CONV_HALO:, SSD_INNER:] = bc_ref[...]
    conv = cb_ref[...]
    for k in range(SSD_CONV):
        off = CONV_HALO - (SSD_CONV - 1) + k
        conv = conv + cw_ref[k:k + 1, :] * xbuf[off:off + CHUNK, :]
    rowi = lax.broadcasted_iota(I32, (CHUNK, 1), 0)
    valid = rowi >= jnp.where(c == 0, LEAD, 0)
    act_ref[...] = jnp.where(valid, conv * _sigmoid(conv), 0.0)

    dtr = dt_ref[...] + dtb_ref[...]
    dt = jnp.maximum(dtr, 0.0) + jnp.log(1.0 + jnp.exp(-jnp.abs(dtr)))
    dt = jnp.where(valid, dt, 0.0)
    adt = dt * (-jnp.exp(alog_ref[...]))
    coli = lax.broadcasted_iota(I32, (1, CHUNK), 1)
    causal = rowi >= coli
    a_cs = jnp.dot(causal.astype(F32), adt, precision=lax.Precision.HIGHEST,
                   preferred_element_type=F32)
    a_t = a_cs.T
    left = coli < SSD_HEAD_DIM

    for g in range(SSD_GROUPS):
        b_g = act_ref[:, SSD_INNER + g * SSD_STATE:SSD_INNER + (g + 1) * SSD_STATE]
        c_g = act_ref[:, SSD_INNER + (SSD_GROUPS + g) * SSD_STATE:
                      SSD_INNER + (SSD_GROUPS + g + 1) * SSD_STATE].astype(BF16)
        cb = lax.dot_general(c_g, b_g.astype(BF16), (((1,), (1,)), ((), ())),
                             preferred_element_type=F32)
        b_gt = b_g.T.astype(BF16)
        st_g = st_ref[g]
        y_off = jnp.dot(c_g, st_g.astype(BF16), preferred_element_type=F32)
        xw_parts, dec_parts = [], []
        for q in range(PAIRS_PER_GROUP):
            pair = g * PAIRS_PER_GROUP + q
            h0 = 2 * pair
            lanes = slice(pair * LANES, (pair + 1) * LANES)
            a0, a1 = a_cs[:, h0:h0 + 1], a_cs[:, h0 + 1:h0 + 2]
            a_exp = jnp.where(left, a0, a1)
            dt_exp = jnp.where(left, dt[:, h0:h0 + 1], dt[:, h0 + 1:h0 + 2])
            xq = act_ref[:, lanes]
            xdt = xq * dt_exp
            m0 = (cb * jnp.exp(jnp.where(causal, a0 - a_t[h0:h0 + 1, :], -jnp.inf))).astype(BF16)
            m1 = (cb * jnp.exp(jnp.where(causal, a1 - a_t[h0 + 1:h0 + 2, :], -jnp.inf))).astype(BF16)
            x_l = jnp.where(left, xdt, 0.0).astype(BF16)
            x_r = jnp.where(left, 0.0, xdt).astype(BF16)
            y_diag = (jnp.dot(m0, x_l, preferred_element_type=F32)
                      + jnp.dot(m1, x_r, preferred_element_type=F32))
            a_last = a_exp[CHUNK - 1:CHUNK, :]
            y_ref[:, lanes] = (y_diag + y_off[:, q * LANES:(q + 1) * LANES] * jnp.exp(a_exp)
                               + dsk_ref[:, lanes] * xq)
            xw_parts.append((xdt * jnp.exp(a_last - a_exp)).astype(BF16))
            dec_parts.append(jnp.exp(a_last))
        xw = jnp.concatenate(xw_parts, axis=-1)
        dec = jnp.concatenate(dec_parts, axis=-1)
        st_ref[g] = st_g * dec + jnp.dot(b_gt, xw, preferred_element_type=F32)

    zz = z_ref[...]
    y = y_ref[...] * (zz * _sigmoid(zz))
    outs = []
    for g in range(SSD_GROUPS):
        yg = y[:, g * GROUP_WIDTH:(g + 1) * GROUP_WIDTH]
        outs.append(yg * lax.rsqrt(jnp.mean(yg * yg, axis=-1, keepdims=True) + EPS))
    yn = jnp.concatenate(outs, axis=-1) * nrm_ref[...]
    slot = lax.rem(c, CPT)
    yn_ref[pl.ds(pl.multiple_of(slot * CHUNK, CHUNK), CHUNK), :] = yn.astype(BF16)

    @pl.when(slot == CPT - 1)
    def _():
        yo = jnp.dot(yn_ref[...], wo_ref[...], preferred_element_type=F32)
        o_ref[...] = (_sigmoid(gs_ref[...]) * yo).astype(BF16)


def _ssd_branch(proj, dt_raw, conv_w, conv_b, dt_bias, a_log, d_skip, ssd_norm, w_ssd_out, bsz, lp):
    rows = proj.shape[0]
    nc = lp // CHUNK
    tiles = nc // CPT
    halo_blocks = CHUNK // CONV_HALO
    chunk_idx = lambda b, c: b * nc + c
    prev_idx = lambda b, c: jnp.maximum((b * nc + c) * halo_blocks - 1, 0)
    const = lambda b, c: (0, 0)
    return pl.pallas_call(
        _ssd_kernel,
        grid=(bsz, nc),
        in_specs=[pl.BlockSpec((CHUNK, SSD_INNER), lambda b, c: (chunk_idx(b, c), COL_Z)),
                  pl.BlockSpec((CHUNK, SSD_INNER), lambda b, c: (chunk_idx(b, c), COL_X)),
                  pl.BlockSpec((CONV_HALO, SSD_INNER), lambda b, c: (prev_idx(b, c), COL_X)),
                  pl.BlockSpec((CHUNK, BC_WIDTH), lambda b, c: (chunk_idx(b, c), COL_BC)),
                  pl.BlockSpec((CONV_HALO, BC_WIDTH), lambda b, c: (prev_idx(b, c), COL_BC)),
                  pl.BlockSpec((CHUNK, LANES), lambda b, c: (chunk_idx(b, c), 0)),
                  pl.BlockSpec((TM, D_MODEL), lambda b, c: (b * tiles + c // CPT, COL_GS)),
                  pl.BlockSpec((SSD_CONV, SSD_INNER + BC_WIDTH), const),
                  pl.BlockSpec((1, SSD_INNER + BC_WIDTH), const),
                  pl.BlockSpec((1, LANES), const),
                  pl.BlockSpec((1, LANES), const),
                  pl.BlockSpec((1, SSD_INNER), const),
                  pl.BlockSpec((1, SSD_INNER), const),
                  pl.BlockSpec((SSD_INNER, D_MODEL), const)],
        out_specs=pl.BlockSpec((TM, D_MODEL), lambda b, c: (b * tiles + c // CPT, 0)),
        out_shape=jax.ShapeDtypeStruct((rows, D_MODEL), BF16),
        scratch_shapes=[pltpu.VMEM((CONV_HALO + CHUNK, SSD_INNER + BC_WIDTH), F32),
                        pltpu.VMEM((CHUNK, SSD_INNER + BC_WIDTH), F32),
                        pltpu.VMEM((CHUNK, SSD_INNER), F32),
                        pltpu.VMEM((SSD_GROUPS, SSD_STATE, GROUP_WIDTH), F32),
                        pltpu.VMEM((TM, SSD_INNER), BF16)],
        compiler_params=_cparams("parallel", "arbitrary"),
        name="ssd_branch",
    )(proj, proj, proj, proj, proj, dt_raw, proj,
      conv_w, conv_b, dt_bias, a_log, d_skip, ssd_norm, w_ssd_out)


def _first_index_of_max(vals, lane):
    m = jnp.max(vals, axis=-1, keepdims=True)
    idx = jnp.min(jnp.where(vals == m, lane, LANES), axis=-1, keepdims=True)
    return m, idx


def _merge_kernel(h_ref, yp_ref, ys_ref, wo_ref, nf_ref, wr_ref, br_ref,
                  h1_ref, hn_ref, route_ref, cnt_ref, carry, *, lp):
    i = pl.program_id(0)

    @pl.when(i == 0)
    def _():
        carry[...] = jnp.zeros_like(carry)

    merged = yp_ref[...].astype(F32) + ys_ref[...].astype(F32)
    h1 = h_ref[...] + jnp.dot(merged.astype(BF16), wo_ref[...], preferred_element_type=F32)
    h1_ref[...] = h1
    hn = h1 * lax.rsqrt(jnp.mean(h1 * h1, axis=-1, keepdims=True) + EPS) * nf_ref[...]
    hn_ref[...] = hn

    logits = jnp.dot(hn, wr_ref[...], precision=lax.Precision.HIGHEST,
                     preferred_element_type=F32) + br_ref[...]
    lane = lax.broadcasted_iota(I32, (1, LANES), 1)
    neg = -jnp.inf
    gl = jnp.where(lane < N_EXPERT_GROUPS, logits, neg)
    gmax, gidx = _first_index_of_max(gl, lane)
    g_p = 1.0 / jnp.sum(jnp.exp(gl - gmax), axis=-1, keepdims=True)
    elo = N_EXPERT_GROUPS + gidx * EXPERTS_PER_GROUP
    el = jnp.where((lane >= elo) & (lane < elo + EXPERTS_PER_GROUP), logits, neg)
    m1, i1 = _first_index_of_max(el, lane)
    m2, i2 = _first_index_of_max(jnp.where(lane == i1, neg, el), lane)
    t = jnp.exp(m2 - m1)
    w1 = g_p / (1.0 + t)
    w2 = g_p * t / (1.0 + t)
    e1 = i1 - N_EXPERT_GROUPS
    e2 = i2 - N_EXPERT_GROUPS

    rowi = lax.broadcasted_iota(I32, (TM, 1), 0)
    valid = (lax.rem(i * TM, lp) + rowi) >= LEAD
    o1 = jnp.where((lane == e1) & valid, 1.0, 0.0)
    o2 = jnp.where((lane == e2) & valid, 1.0, 0.0)
    both = o1 + o2
    earlier = (rowi > lax.broadcasted_iota(I32, (1, TM), 1)).astype(BF16)
    before = jnp.dot(earlier, both.astype(BF16), preferred_element_type=F32) + carry[...]
    r1 = jnp.sum(o1 * before, axis=-1, keepdims=True)
    r2 = jnp.sum(o2 * before, axis=-1, keepdims=True)
    carry[...] = carry[...] + jnp.sum(both, axis=0, keepdims=True)
    cnt_ref[...] = carry[...]

    route = jnp.where(lane == 0, e1.astype(F32), 0.0)
    route = jnp.where(lane == 1, e2.astype(F32), route)
    route = jnp.where(lane == 2, r1, route)
    route = jnp.where(lane == 3, r2, route)
    route = jnp.where(lane == 4, w1, route)
    route = jnp.where(lane == 5, w2, route)
    route_ref[...] = route


def _merge_router(hp, yp, ys, w_out, norm_ffn, w_router, b_router, lp):
    rows = hp.shape[0]
    tile = lambda i: (i, 0)
    const = lambda i: (0, 0)
    return pl.pallas_call(
        functools.partial(_merge_kernel, lp=lp),
        grid=(rows // TM,),
        in_specs=[pl.BlockSpec((TM, D_MODEL), tile),
                  pl.BlockSpec((TM, D_MODEL), tile),
                  pl.BlockSpec((TM, D_MODEL), tile),
                  pl.BlockSpec((D_MODEL, D_MODEL), const),
                  pl.BlockSpec((1, D_MODEL), const),
                  pl.BlockSpec((D_MODEL, LANES), const),
                  pl.BlockSpec((1, LANES), const)],
        out_specs=[pl.BlockSpec((TM, D_MODEL), tile),
                   pl.BlockSpec((TM, D_MODEL), tile),
                   pl.BlockSpec((TM, LANES), tile),
                   pl.BlockSpec((1, LANES), const)],
        out_shape=[jax.ShapeDtypeStruct((rows, D_MODEL), F32),
                   jax.ShapeDtypeStruct((rows, D_MODEL), F32),
                   jax.ShapeDtypeStruct((rows, LANES), F32),
                   jax.ShapeDtypeStruct((1, LANES), F32)],
        scratch_shapes=[pltpu.VMEM((1, LANES), F32)],
        compiler_params=_cparams("arbitrary"),
        name="merge_router",
    )(hp, yp, ys, w_out, norm_ffn, w_router, b_router)


def _row_copy(src, src_row, dst, dst_row, sem):
    return pltpu.make_async_copy(src.at[pl.ds(src_row, 1)], dst.at[pl.ds(dst_row, 1)], sem)


def _dispatch_kernel(dest_ref, hn_ref, xs_in, xs_out, sem):
    del xs_in
    start = jnp.where(pl.program_id(1) == 0, LEAD, 0)

    def issue(r, carry):
        blk, off = r // CHUNK, lax.rem(r, CHUNK)
        for k in range(2):
            _row_copy(hn_ref, r, xs_out, dest_ref[blk, k, off], sem).start()
        return carry

    def drain(r, carry):
        for k in range(2):
            _row_copy(hn_ref, 0, xs_out, 0, sem).wait()
        return carry

    lax.fori_loop(start, TM, issue, 0)
    lax.fori_loop(start, TM, drain, 0)


def _dispatch(dest, hn, xs_zero, bsz, lp):
    tiles = lp // TM
    return pl.pallas_call(
        _dispatch_kernel,
        grid=(bsz, tiles),
        in_specs=[pl.BlockSpec((CPT, 2, CHUNK), lambda b, c: (b * tiles + c, 0, 0),
                               memory_space=pltpu.SMEM),
                  pl.BlockSpec((TM, D_MODEL), lambda b, c: (b * tiles + c, 0)),
                  pl.BlockSpec(memory_space=pl.ANY)],
        out_specs=pl.BlockSpec(memory_space=pl.ANY),
        out_shape=jax.ShapeDtypeStruct(xs_zero.shape, xs_zero.dtype),
        scratch_shapes=[pltpu.SemaphoreType.DMA(())],
        input_output_aliases={2: 0},
        compiler_params=_cparams("arbitrary", "arbitrary"),
        name="moe_dispatch",
    )(dest, hn, xs_zero)


def _expert_kernel(be_ref, nused_ref, x_ref, wgu_ref, wdn_ref, y_ref, wgu16, wdn16):
    i = pl.program_id(0)

    @pl.when(i < nused_ref[0])
    def _():
        @pl.when((i == 0) | (be_ref[i] != be_ref[jnp.maximum(i - 1, 0)]))
        def _():
            wgu16[...] = wgu_ref[0].astype(BF16)
            wdn16[...] = wdn_ref[0].astype(BF16)

        gu = jnp.dot(x_ref[...].astype(BF16), wgu16[...], preferred_element_type=F32)
        gate, up = gu[:, :D_EXPERT], gu[:, D_EXPERT:]
        act = gate * _sigmoid(gate) * up
        y_ref[...] = jnp.dot(act.astype(BF16), wdn16[...], preferred_element_type=F32)

    @pl.when(i >= nused_ref[0])
    def _():
        y_ref[...] = jnp.zeros_like(y_ref)


def _experts(block_e, nused, xs, w_gu, w_dn):
    nb = xs.shape[0] // MOE_BLOCK
    blk = lambda i, be, nu: (jnp.minimum(i, nu[0] - 1), 0)
    oblk = lambda i, be, nu: (i, 0)
    wsel = lambda i, be, nu: (be[jnp.minimum(i, nu[0] - 1)], 0, 0)
    return pl.pallas_call(
        _expert_kernel,
        grid_spec=pltpu.PrefetchScalarGridSpec(
            num_scalar_prefetch=2,
            grid=(nb,),
            in_specs=[pl.BlockSpec((MOE_BLOCK, D_MODEL), blk),
                      pl.BlockSpec((1, D_MODEL, 2 * D_EXPERT), wsel),
                      pl.BlockSpec((1, D_EXPERT, D_MODEL), wsel)],
            out_specs=pl.BlockSpec((MOE_BLOCK, D_MODEL), oblk),
            scratch_shapes=[pltpu.VMEM((D_MODEL, 2 * D_EXPERT), BF16),
                            pltpu.VMEM((D_EXPERT, D_MODEL), BF16)]),
        out_shape=jax.ShapeDtypeStruct(xs.shape, F32),
        compiler_params=_cparams("arbitrary"),
        name="moe_experts",
    )(block_e, nused, xs, w_gu, w_dn)


def _combine_kernel(dest_ref, h1_ref, rt_ref, nf_ref, y_hbm, o_ref, ya, yb, sem):
    def issue(r, carry):
        _row_copy(y_hbm, dest_ref[0, 0, r], ya, r, sem.at[0]).start()
        _row_copy(y_hbm, dest_ref[0, 1, r], yb, r, sem.at[1]).start()
        return carry

    def drain(r, carry):
        _row_copy(y_hbm, 0, ya, 0, sem.at[0]).wait()
        _row_copy(y_hbm, 0, yb, 0, sem.at[1]).wait()
        return carry

    lax.fori_loop(0, CHUNK, issue, 0)
    lax.fori_loop(0, CHUNK, drain, 0)
    rt = rt_ref[...]
    h2 = h1_ref[...] + rt[:, 4:5] * ya[...] + rt[:, 5:6] * yb[...]
    o_ref[0] = h2 * lax.rsqrt(jnp.mean(h2 * h2, axis=-1, keepdims=True) + EPS) * nf_ref[...]


def _combine(dest, h1, route, norm_final, ys, bsz, lp, seq):
    nc = lp // CHUNK
    real = lambda b, j: b * nc + 1 + j
    return pl.pallas_call(
        _combine_kernel,
        grid=(bsz, seq // CHUNK),
        in_specs=[pl.BlockSpec((1, 2, CHUNK), lambda b, j: (real(b, j), 0, 0),
                               memory_space=pltpu.SMEM),
                  pl.BlockSpec((CHUNK, D_MODEL), lambda b, j: (real(b, j), 0)),
                  pl.BlockSpec((CHUNK, LANES), lambda b, j: (real(b, j), 0)),
                  pl.BlockSpec((1, D_MODEL), lambda b, j: (0, 0)),
                  pl.BlockSpec(memory_space=pl.ANY)],
        out_specs=pl.BlockSpec((1, CHUNK, D_MODEL), lambda b, j: (b, j, 0)),
        out_shape=jax.ShapeDtypeStruct((bsz, seq, D_MODEL), F32),
        scratch_shapes=[pltpu.VMEM((CHUNK, D_MODEL), F32),
                        pltpu.VMEM((CHUNK, D_MODEL), F32),
                        pltpu.SemaphoreType.DMA((2,))],
        compiler_params=_cparams("arbitrary", "arbitrary"),
        name="moe_combine",
    )(dest, h1, route, norm_final, ys)


def _row(v):
    return v.reshape(1, -1).astype(F32)


def _pad_lanes(v):
    return jnp.pad(v, ((0, 0), (0, LANES - v.shape[1])))


def kernel(x, meta_tokens, norm_mix, w_in, pool_w, pool_scale, conv_w, conv_b, dt_bias, a_log, d_skip,
           ssd_norm, w_pool_out, w_ssd_out, w_out, norm_ffn, w_router_group, b_router_group,
           w_router_expert, b_router_expert, w_gate_up, w_down, norm_final):
    bsz, seq, _ = x.shape
    lp = LEAD + N_META + seq
    assert lp % TM == 0 and seq % CHUNK == 0
    rows = bsz * lp

    meta = jnp.broadcast_to(meta_tokens.astype(x.dtype)[None], (bsz, N_META, D_MODEL))
    hp = jnp.concatenate([jnp.zeros((bsz, LEAD, D_MODEL), x.dtype), meta, x], axis=1)
    hp = hp.reshape(rows, D_MODEL)

    wi = w_in[0]
    o_z, o_xbc, o_dt, o_gp, o_gs = 1024, 3072, 6144, 6176, 7200
    w_main = jnp.concatenate([wi[:, o_z:o_xbc], wi[:, o_xbc:o_dt], wi[:, :o_z],
                              wi[:, o_gp:o_gs], wi[:, o_gs:]], axis=1).astype(BF16)
    w_dt = _pad_lanes(wi[:, o_dt:o_gp]).astype(BF16)
    gain = _row(norm_mix[0])
    proj = _in_proj(hp, gain, w_main, PROJ_TN)
    dt_raw = _in_proj(hp, gain, w_dt, LANES)

    yp = _pool_branch(proj, pool_w[0].astype(BF16), _row(pool_scale[0]),
                      w_pool_out[0].astype(BF16), lp)
    ys = _ssd_branch(proj, dt_raw, conv_w[0].astype(F32), _row(conv_b[0]),
                     _pad_lanes(_row(dt_bias[0])), _pad_lanes(_row(a_log[0])),
                     _row(jnp.repeat(d_skip[0], SSD_HEAD_DIM)), _row(ssd_norm[0]),
                     w_ssd_out[0].astype(BF16), bsz, lp)

    w_router = _pad_lanes(jnp.concatenate([w_router_group[0], w_router_expert[0]], axis=1).astype(F32))
    b_router = _pad_lanes(jnp.concatenate([_row(b_router_group[0]), _row(b_router_expert[0])], axis=1))
    h1, hn, route, counts = _merge_router(hp, yp, ys, w_out[0].astype(BF16), _row(norm_ffn[0]),
                                          w_router, b_router, lp)

    n_assign = 2 * bsz * (N_META + seq)
    nb = -(-n_assign // MOE_BLOCK) + N_EXPERTS
    cnt = counts[0, :N_EXPERTS].astype(I32)
    pcnt = (cnt + MOE_BLOCK - 1) // MOE_BLOCK * MOE_BLOCK
    pends = jnp.cumsum(pcnt)
    pstarts = pends - pcnt
    experts = route[:, 0:2].astype(I32)
    dest = pstarts[experts] + route[:, 2:4].astype(I32)
    dest = dest.reshape(rows // CHUNK, CHUNK, 2).transpose(0, 2, 1)
    block_e = jnp.minimum(jnp.searchsorted(pends, jnp.arange(nb, dtype=I32) * MOE_BLOCK, side='right'),
                          N_EXPERTS - 1).astype(I32)
    nused = (pends[-1:] // MOE_BLOCK).astype(I32)

    xs = _dispatch(dest, hn, jnp.zeros((nb * MOE_BLOCK, D_MODEL), F32), bsz, lp)
    yexp = _experts(block_e, nused, xs, w_gate_up[0], w_down[0])
    return _combine(dest, h1, route, _row(norm_final), yexp, bsz, lp, seq)
```

```python
import functools

import jax
import jax.numpy as jnp
from jax import lax
from jax.experimental import pallas as pl
from jax.experimental.pallas import tpu as pltpu

F32 = jnp.float32
BF16 = jnp.bfloat16
I32 = jnp.int32

D_MODEL = 1024
N_META = 16
EPS = 1e-6
CHUNK = 128
LEAD = CHUNK - N_META
CPT = 5
TM = CPT * CHUNK
POOL_WINDOWS = (2, 4, 8, 16)
POOL_GROUP = D_MODEL // len(POOL_WINDOWS)
POOL_HALO = 16
SSD_INNER = 2 * D_MODEL
SSD_HEADS = 32
SSD_HEAD_DIM = 64
SSD_GROUPS = 4
SSD_STATE = 128
SSD_CONV = 4
CONV_HALO = 8
BC_WIDTH = 2 * SSD_GROUPS * SSD_STATE
GROUP_WIDTH = SSD_INNER // SSD_GROUPS
PAIRS_PER_GROUP = GROUP_WIDTH // 128
N_EXPERT_GROUPS = 8
EXPERTS_PER_GROUP = 8
N_EXPERTS = 64
D_EXPERT = 512
MOE_BLOCK = 256
LANES = 128
DMA_UNROLL = 8

COL_Z, COL_X, COL_BC, COL_U, COL_GP, COL_GS = 0, 1, 4, 5, 6, 7
PROJ_WIDTH = 8 * D_MODEL
PROJ_TN = 2048

VMEM_LIMIT = 48 * 1024 * 1024


def _cparams(*sem):
    return pltpu.CompilerParams(dimension_semantics=sem, vmem_limit_bytes=VMEM_LIMIT)


def _sigmoid(x):
    return 1.0 / (1.0 + jnp.exp(-x))


def _inproj_kernel(h_ref, g_ref, w_ref, o_ref):
    x = h_ref[...]
    hn = x * lax.rsqrt(jnp.mean(x * x, axis=-1, keepdims=True) + EPS) * g_ref[...]
    o_ref[...] = jnp.dot(hn.astype(BF16), w_ref[...], preferred_element_type=F32)


def _in_proj(hp, gain, w, tn):
    rows, n = hp.shape[0], w.shape[1]
    return pl.pallas_call(
        _inproj_kernel,
        grid=(n // tn, rows // TM),
        in_specs=[pl.BlockSpec((TM, D_MODEL), lambda j, i: (i, 0)),
                  pl.BlockSpec((1, D_MODEL), lambda j, i: (0, 0)),
                  pl.BlockSpec((D_MODEL, tn), lambda j, i: (0, j))],
        out_specs=pl.BlockSpec((TM, tn), lambda j, i: (i, j)),
        out_shape=jax.ShapeDtypeStruct((rows, n), F32),
        compiler_params=_cparams("parallel", "parallel"),
        name="in_proj",
    )(hp, gain, w)


def _pool_kernel(u_ref, uprev_ref, gp_ref, pw_ref, ps_ref, wo_ref, o_ref, ubuf, *, lp):
    i = pl.program_id(0)
    ubuf[0:POOL_HALO, :] = uprev_ref[...]
    ubuf[POOL_HALO:POOL_HALO + TM, :] = u_ref[...]
    row = lax.rem(i * TM, lp) + lax.broadcasted_iota(I32, (TM, 1), 0)
    tpos = row - LEAD
    parts = []
    for k, w in enumerate(POOL_WINDOWS):
        lo = k * POOL_GROUP
        cur = ubuf[POOL_HALO:POOL_HALO + TM, lo:lo + POOL_GROUP]
        acc = cur
        for s in range(1, w):
            acc = acc + ubuf[POOL_HALO - s:POOL_HALO - s + TM, lo:lo + POOL_GROUP]
        cnt = jnp.clip(tpos + 1, 1, w).astype(F32)
        d = acc / cnt - cur
        parts.append(jnp.dot(d.astype(BF16), pw_ref[k], preferred_element_type=F32))
    pm = jnp.concatenate(parts, axis=-1) * ps_ref[...]
    y = jnp.dot(pm.astype(BF16), wo_ref[...], preferred_element_type=F32)
    o_ref[...] = (_sigmoid(gp_ref[...]) * y).astype(BF16)


def _pool_branch(proj, pool_w, pool_scale, w_pool_out, lp):
    rows = proj.shape[0]
    halo_blocks = TM // POOL_HALO
    return pl.pallas_call(
        functools.partial(_pool_kernel, lp=lp),
        grid=(rows // TM,),
        in_specs=[pl.BlockSpec((TM, D_MODEL), lambda i: (i, COL_U)),
                  pl.BlockSpec((POOL_HALO, D_MODEL),
                               lambda i: (jnp.maximum(i * halo_blocks - 1, 0), COL_U)),
                  pl.BlockSpec((TM, D_MODEL), lambda i: (i, COL_GP)),
                  pl.BlockSpec((len(POOL_WINDOWS), POOL_GROUP, POOL_GROUP), lambda i: (0, 0, 0)),
                  pl.BlockSpec((1, D_MODEL), lambda i: (0, 0)),
                  pl.BlockSpec((D_MODEL, D_MODEL), lambda i: (0, 0))],
        out_specs=pl.BlockSpec((TM, D_MODEL), lambda i: (i, 0)),
        out_shape=jax.ShapeDtypeStruct((rows, D_MODEL), BF16),
        scratch_shapes=[pltpu.VMEM((POOL_HALO + TM, D_MODEL), F32)],
        compiler_params=_cparams("parallel"),
        name="pool_branch",
    )(proj, proj, proj, pool_w, pool_scale, w_pool_out)


def _ssd_kernel(z_ref, x_ref, xprev_ref, bc_ref, bcprev_ref, dt_ref, gs_ref,
                cw_ref, cb_ref, dtb_ref, alog_ref, dsk_ref, nrm_ref, wo_ref,
                o_ref, xbuf, act_ref, y_ref, st_ref, yn_ref):
    c = pl.program_id(1)

    @pl.when(c == 0)
    def _():
        st_ref[...] = jnp.zeros_like(st_ref)

    xbuf[0:CONV_HALO, 0:SSD_INNER] = xprev_ref[...]
    xbuf[0:CONV_HALO, SSD_INNER:] = bcprev_ref[...]
    xbuf[CONV_HALO:, 0:SSD_INNER] = x_ref[...]
    xbuf[CONV_HALO:, SSD_INNER:] = bc_ref[...]
    conv = cb_ref[...]
    for k in range(SSD_CONV):
        off = CONV_HALO - (SSD_CONV - 1) + k
        conv = conv + cw_ref[k:k + 1, :] * xbuf[off:off + CHUNK, :]
    rowi = lax.broadcasted_iota(I32, (CHUNK, 1), 0)
    valid = rowi >= jnp.where(c == 0, LEAD, 0)
    act_ref[...] = jnp.where(valid, conv * _sigmoid(conv), 0.0)

    dtr = dt_ref[...] + dtb_ref[...]
    dt = jnp.maximum(dtr, 0.0) + jnp.log(1.0 + jnp.exp(-jnp.abs(dtr)))
    dt = jnp.where(valid, dt, 0.0)
    adt = dt * (-jnp.exp(alog_ref[...]))
    coli = lax.broadcasted_iota(I32, (1, CHUNK), 1)
    causal = rowi >= coli
    a_cs = jnp.dot(causal.astype(F32), adt, precision=lax.Precision.HIGHEST,
                   preferred_element_type=F32)
    a_t = a_cs.T
    left = coli < SSD_HEAD_DIM

    for g in range(SSD_GROUPS):
        b_g = act_ref[:, SSD_INNER + g * SSD_STATE:SSD_INNER + (g + 1) * SSD_STATE]
        c_g = act_ref[:, SSD_INNER + (SSD_GROUPS + g) * SSD_STATE:
                      SSD_INNER + (SSD_GROUPS + g + 1) * SSD_STATE].astype(BF16)
        cb = lax.dot_general(c_g, b_g.astype(BF16), (((1,), (1,)), ((), ())),
                             preferred_element_type=F32)
        b_gt = b_g.T.astype(BF16)
        st_g = st_ref[g]
        y_off = jnp.dot(c_g, st_g.astype(BF16), preferred_element_type=F32)
        xw_parts, dec_parts = [], []
        for q in range(PAIRS_PER_GROUP):
            pair = g * PAIRS_PER_GROUP + q
            h0 = 2 * pair
            lanes = slice(pair * LANES, (pair + 1) * LANES)
            a0, a1 = a_cs[:, h0:h0 + 1], a_cs[:, h0 + 1:h0 + 2]
            a_exp = jnp.where(left, a0, a1)
            dt_exp = jnp.where(left, dt[:, h0:h0 + 1], dt[:, h0 + 1:h0 + 2])
            xq = act_ref[:, lanes]
            xdt = xq * dt_exp
            m0 = (cb * jnp.exp(jnp.where(causal, a0 - a_t[h0:h0 + 1, :], -jnp.inf))).astype(BF16)
            m1 = (cb * jnp.exp(jnp.where(causal, a1 - a_t[h0 + 1:h0 + 2, :], -jnp.inf))).astype(BF16)
            x_l = jnp.where(left, xdt, 0.0).astype(BF16)
            x_r = jnp.where(left, 0.0, xdt).astype(BF16)
            y_diag = (jnp.dot(m0, x_l, preferred_element_type=F32)
                      + jnp.dot(m1, x_r, preferred_element_type=F32))
            a_last = a_exp[CHUNK - 1:CHUNK, :]
            y_ref[:, lanes] = (y_diag + y_off[:, q * LANES:(q + 1) * LANES] * jnp.exp(a_exp)
                               + dsk_ref[:, lanes] * xq)
            xw_parts.append((xdt * jnp.exp(a_last - a_exp)).astype(BF16))
            dec_parts.append(jnp.exp(a_last))
        xw = jnp.concatenate(xw_parts, axis=-1)
        dec = jnp.concatenate(dec_parts, axis=-1)
        st_ref[g] = st_g * dec + jnp.dot(b_gt, xw, preferred_element_type=F32)

    zz = z_ref[...]
    y = y_ref[...] * (zz * _sigmoid(zz))
    outs = []
    for g in range(SSD_GROUPS):
        yg = y[:, g * GROUP_WIDTH:(g + 1) * GROUP_WIDTH]
        outs.append(yg * lax.rsqrt(jnp.mean(yg * yg, axis=-1, keepdims=True) + EPS))
    yn = jnp.concatenate(outs, axis=-1) * nrm_ref[...]
    slot = lax.rem(c, CPT)
    yn_ref[pl.ds(pl.multiple_of(slot * CHUNK, CHUNK), CHUNK), :] = yn.astype(BF16)

    @pl.when(slot == CPT - 1)
    def _():
        yo = jnp.dot(yn_ref[...], wo_ref[...], preferred_element_type=F32)
        o_ref[...] = (_sigmoid(gs_ref[...]) * yo).astype(BF16)


def _ssd_branch(proj, dt_raw, conv_w, conv_b, dt_bias, a_log, d_skip, ssd_norm, w_ssd_out, bsz, lp):
    rows = proj.shape[0]
    nc = lp // CHUNK
    tiles = nc // CPT
    halo_blocks = CHUNK // CONV_HALO
    chunk_idx = lambda b, c: b * nc + c
    prev_idx = lambda b, c: jnp.maximum((b * nc + c) * halo_blocks - 1, 0)
    const = lambda b, c: (0, 0)
    return pl.pallas_call(
        _ssd_kernel,
        grid=(bsz, nc),
        in_specs=[pl.BlockSpec((CHUNK, SSD_INNER), lambda b, c: (chunk_idx(b, c), COL_Z)),
                  pl.BlockSpec((CHUNK, SSD_INNER), lambda b, c: (chunk_idx(b, c), COL_X)),
                  pl.BlockSpec((CONV_HALO, SSD_INNER), lambda b, c: (prev_idx(b, c), COL_X)),
                  pl.BlockSpec((CHUNK, BC_WIDTH), lambda b, c: (chunk_idx(b, c), COL_BC)),
                  pl.BlockSpec((CONV_HALO, BC_WIDTH), lambda b, c: (prev_idx(b, c), COL_BC)),
                  pl.BlockSpec((CHUNK, LANES), lambda b, c: (chunk_idx(b, c), 0)),
                  pl.BlockSpec((TM, D_MODEL), lambda b, c: (b * tiles + c // CPT, COL_GS)),
                  pl.BlockSpec((SSD_CONV, SSD_INNER + BC_WIDTH), const),
                  pl.BlockSpec((1, SSD_INNER + BC_WIDTH), const),
                  pl.BlockSpec((1, LANES), const),
                  pl.BlockSpec((1, LANES), const),
                  pl.BlockSpec((1, SSD_INNER), const),
                  pl.BlockSpec((1, SSD_INNER), const),
                  pl.BlockSpec((SSD_INNER, D_MODEL), const)],
        out_specs=pl.BlockSpec((TM, D_MODEL), lambda b, c: (b * tiles + c // CPT, 0)),
        out_shape=jax.ShapeDtypeStruct((rows, D_MODEL), BF16),
        scratch_shapes=[pltpu.VMEM((CONV_HALO + CHUNK, SSD_INNER + BC_WIDTH), F32),
                        pltpu.VMEM((CHUNK, SSD_INNER + BC_WIDTH), F32),
                        pltpu.VMEM((CHUNK, SSD_INNER), F32),
                        pltpu.VMEM((SSD_GROUPS, SSD_STATE, GROUP_WIDTH), F32),
                        pltpu.VMEM((TM, SSD_INNER), BF16)],
        compiler_params=_cparams("parallel", "arbitrary"),
        name="ssd_branch",
    )(proj, proj, proj, proj, proj, dt_raw, proj,
      conv_w, conv_b, dt_bias, a_log, d_skip, ssd_norm, w_ssd_out)


def _first_index_of_max(vals, lane):
    m = jnp.max(vals, axis=-1, keepdims=True)
    idx = jnp.min(jnp.where(vals == m, lane, LANES), axis=-1, keepdims=True)
    return m, idx


def _merge_kernel(h_ref, yp_ref, ys_ref, wo_ref, nf_ref, wr_ref, br_ref,
                  h1_ref, hn_ref, route_ref, cnt_ref, carry, *, lp):
    i = pl.program_id(0)

    @pl.when(i == 0)
    def _():
        carry[...] = jnp.zeros_like(carry)

    merged = yp_ref[...].astype(F32) + ys_ref[...].astype(F32)
    h1 = h_ref[...] + jnp.dot(merged.astype(BF16), wo_ref[...], preferred_element_type=F32)
    h1_ref[...] = h1
    hn = h1 * lax.rsqrt(jnp.mean(h1 * h1, axis=-1, keepdims=True) + EPS) * nf_ref[...]
    hn_ref[...] = hn

    logits = jnp.dot(hn, wr_ref[...], precision=lax.Precision.HIGHEST,
                     preferred_element_type=F32) + br_ref[...]
    lane = lax.broadcasted_iota(I32, (1, LANES), 1)
    neg = -jnp.inf
    gl = jnp.where(lane < N_EXPERT_GROUPS, logits, neg)
    gmax, gidx = _first_index_of_max(gl, lane)
    g_p = 1.0 / jnp.sum(jnp.exp(gl - gmax), axis=-1, keepdims=True)
    elo = N_EXPERT_GROUPS + gidx * EXPERTS_PER_GROUP
    el = jnp.where((lane >= elo) & (lane < elo + EXPERTS_PER_GROUP), logits, neg)
    m1, i1 = _first_index_of_max(el, lane)
    m2, i2 = _first_index_of_max(jnp.where(lane == i1, neg, el), lane)
    t = jnp.exp(m2 - m1)
    w1 = g_p / (1.0 + t)
    w2 = g_p * t / (1.0 + t)
    e1 = i1 - N_EXPERT_GROUPS
    e2 = i2 - N_EXPERT_GROUPS

    rowi = lax.broadcasted_iota(I32, (TM, 1), 0)
    valid = (lax.rem(i * TM, lp) + rowi) >= LEAD
    o1 = jnp.where((lane == e1) & valid, 1.0, 0.0)
    o2 = jnp.where((lane == e2) & valid, 1.0, 0.0)
    both = o1 + o2
    earlier = (rowi > lax.broadcasted_iota(I32, (1, TM), 1)).astype(BF16)
    before = jnp.dot(earlier, both.astype(BF16), preferred_element_type=F32) + carry[...]
    r1 = jnp.sum(o1 * before, axis=-1, keepdims=True)
    r2 = jnp.sum(o2 * before, axis=-1, keepdims=True)
    carry[...] = carry[...] + jnp.sum(both, axis=0, keepdims=True)
    cnt_ref[...] = carry[...]

    route = jnp.where(lane == 0, e1.astype(F32), 0.0)
    route = jnp.where(lane == 1, e2.astype(F32), route)
    route = jnp.where(lane == 2, r1, route)
    route = jnp.where(lane == 3, r2, route)
    route = jnp.where(lane == 4, w1, route)
    route = jnp.where(lane == 5, w2, route)
    route_ref[...] = route


def _merge_router(hp, yp, ys, w_out, norm_ffn, w_router, b_router, lp):
    rows = hp.shape[0]
    tile = lambda i: (i, 0)
    const = lambda i: (0, 0)
    return pl.pallas_call(
        functools.partial(_merge_kernel, lp=lp),
        grid=(rows // TM,),
        in_specs=[pl.BlockSpec((TM, D_MODEL), tile),
                  pl.BlockSpec((TM, D_MODEL), tile),
                  pl.BlockSpec((TM, D_MODEL), tile),
                  pl.BlockSpec((D_MODEL, D_MODEL), const),
                  pl.BlockSpec((1, D_MODEL), const),
                  pl.BlockSpec((D_MODEL, LANES), const),
                  pl.BlockSpec((1, LANES), const)],
        out_specs=[pl.BlockSpec((TM, D_MODEL), tile),
                   pl.BlockSpec((TM, D_MODEL), tile),
                   pl.BlockSpec((TM, LANES), tile),
                   pl.BlockSpec((1, LANES), const)],
        out_shape=[jax.ShapeDtypeStruct((rows, D_MODEL), F32),
                   jax.ShapeDtypeStruct((rows, D_MODEL), F32),
                   jax.ShapeDtypeStruct((rows, LANES), F32),
                   jax.ShapeDtypeStruct((1, LANES), F32)],
        scratch_shapes=[pltpu.VMEM((1, LANES), F32)],
        compiler_params=_cparams("arbitrary"),
        name="merge_router",
    )(hp, yp, ys, w_out, norm_ffn, w_router, b_router)


def _row_copy(src, src_row, dst, dst_row, sem):
    return pltpu.make_async_copy(src.at[pl.ds(src_row, 1)], dst.at[pl.ds(dst_row, 1)], sem)


def _wait_rows(buf, nrows, sem):
    view = buf.at[pl.ds(0, nrows)]
    pltpu.make_async_copy(view, view, sem).wait()


def _dispatch_kernel(dest_ref, hn_ref, xs_in, xs_out, sem):
    del xs_in
    first = pl.program_id(1) == 0

    def issue(blk, lo):
        def body(r, carry):
            for k in range(2):
                _row_copy(hn_ref, blk * CHUNK + r, xs_out, dest_ref[blk, k, r], sem).start()
            return carry
        lax.fori_loop(lo, CHUNK, body, 0, unroll=DMA_UNROLL)

    @pl.when(first)
    def _():
        issue(0, LEAD)

    @pl.when(jnp.logical_not(first))
    def _():
        issue(0, 0)

    for blk in range(1, CPT):
        issue(blk, 0)

    @pl.when(first)
    def _():
        for _k in range(2):
            _wait_rows(hn_ref, TM - LEAD, sem)

    @pl.when(jnp.logical_not(first))
    def _():
        for _k in range(2):
            _wait_rows(hn_ref, TM, sem)


def _dispatch(dest, hn, xs_zero, bsz, lp):
    tiles = lp // TM
    return pl.pallas_call(
        _dispatch_kernel,
        grid=(bsz, tiles),
        in_specs=[pl.BlockSpec((CPT, 2, CHUNK), lambda b, c: (b * tiles + c, 0, 0),
                               memory_space=pltpu.SMEM),
                  pl.BlockSpec((TM, D_MODEL), lambda b, c: (b * tiles + c, 0)),
                  pl.BlockSpec(memory_space=pl.ANY)],
        out_specs=pl.BlockSpec(memory_space=pl.ANY),
        out_shape=jax.ShapeDtypeStruct(xs_zero.shape, xs_zero.dtype),
        scratch_shapes=[pltpu.SemaphoreType.DMA(())],
        input_output_aliases={2: 0},
        compiler_params=_cparams("arbitrary", "arbitrary"),
        name="moe_dispatch",
    )(dest, hn, xs_zero)


def _expert_kernel(be_ref, nused_ref, x_ref, wgu_ref, wdn_ref, y_ref, wgu16, wdn16):
    i = pl.program_id(0)

    @pl.when(i < nused_ref[0])
    def _():
        @pl.when((i == 0) | (be_ref[i] != be_ref[jnp.maximum(i - 1, 0)]))
        def _():
            wgu16[...] = wgu_ref[0].astype(BF16)
            wdn16[...] = wdn_ref[0].astype(BF16)

        gu = jnp.dot(x_ref[...].astype(BF16), wgu16[...], preferred_element_type=F32)
        gate, up = gu[:, :D_EXPERT], gu[:, D_EXPERT:]
        act = gate * _sigmoid(gate) * up
        y_ref[...] = jnp.dot(act.astype(BF16), wdn16[...], preferred_element_type=F32)

    @pl.when(i >= nused_ref[0])
    def _():
        y_ref[...] = jnp.zeros_like(y_ref)


def _experts(block_e, nused, xs, w_gu, w_dn):
    nb = xs.shape[0] // MOE_BLOCK
    blk = lambda i, be, nu: (jnp.minimum(i, nu[0] - 1), 0)
    oblk = lambda i, be, nu: (i, 0)
    wsel = lambda i, be, nu: (be[jnp.minimum(i, nu[0] - 1)], 0, 0)
    return pl.pallas_call(
        _expert_kernel,
        grid_spec=pltpu.PrefetchScalarGridSpec(
            num_scalar_prefetch=2,
            grid=(nb,),
            in_specs=[pl.BlockSpec((MOE_BLOCK, D_MODEL), blk),
                      pl.BlockSpec((1, D_MODEL, 2 * D_EXPERT), wsel),
                      pl.BlockSpec((1, D_EXPERT, D_MODEL), wsel)],
            out_specs=pl.BlockSpec((MOE_BLOCK, D_MODEL), oblk),
            scratch_shapes=[pltpu.VMEM((D_MODEL, 2 * D_EXPERT), BF16),
                            pltpu.VMEM((D_EXPERT, D_MODEL), BF16)]),
        out_shape=jax.ShapeDtypeStruct(xs.shape, F32),
        compiler_params=_cparams("arbitrary"),
        name="moe_experts",
    )(block_e, nused, xs, w_gu, w_dn)


def _combine_kernel(dest_ref, h1_ref, rt_ref, nf_ref, y_hbm, o_ref, ya, yb, sem):
    def issue(r, carry):
        _row_copy(y_hbm, dest_ref[0, 0, r], ya, r, sem.at[0]).start()
        _row_copy(y_hbm, dest_ref[0, 1, r], yb, r, sem.at[1]).start()
        return carry

    lax.fori_loop(0, CHUNK, issue, 0, unroll=DMA_UNROLL)
    _wait_rows(ya, CHUNK, sem.at[0])
    _wait_rows(yb, CHUNK, sem.at[1])
    rt = rt_ref[...]
    h2 = h1_ref[...] + rt[:, 4:5] * ya[...] + rt[:, 5:6] * yb[...]
    o_ref[0] = h2 * lax.rsqrt(jnp.mean(h2 * h2, axis=-1, keepdims=True) + EPS) * nf_ref[...]


def _combine(dest, h1, route, norm_final, ys, bsz, lp, seq):
    nc = lp // CHUNK
    real = lambda b, j: b * nc + 1 + j
    return pl.pallas_call(
        _combine_kernel,
        grid=(bsz, seq // CHUNK),
        in_specs=[pl.BlockSpec((1, 2, CHUNK), lambda b, j: (real(b, j), 0, 0),
                               memory_space=pltpu.SMEM),
                  pl.BlockSpec((CHUNK, D_MODEL), lambda b, j: (real(b, j), 0)),
                  pl.BlockSpec((CHUNK, LANES), lambda b, j: (real(b, j), 0)),
                  pl.BlockSpec((1, D_MODEL), lambda b, j: (0, 0)),
                  pl.BlockSpec(memory_space=pl.ANY)],
        out_specs=pl.BlockSpec((1, CHUNK, D_MODEL), lambda b, j: (b, j, 0)),
        out_shape=jax.ShapeDtypeStruct((bsz, seq, D_MODEL), F32),
        scratch_shapes=[pltpu.VMEM((CHUNK, D_MODEL), F32),
                        pltpu.VMEM((CHUNK, D_MODEL), F32),
                        pltpu.SemaphoreType.DMA((2,))],
        compiler_params=_cparams("arbitrary", "arbitrary"),
        name="moe_combine",
    )(dest, h1, route, norm_final, ys)


def _row(v):
    return v.reshape(1, -1).astype(F32)


def _pad_lanes(v):
    return jnp.pad(v, ((0, 0), (0, LANES - v.shape[1])))


def kernel(x, meta_tokens, norm_mix, w_in, pool_w, pool_scale, conv_w, conv_b, dt_bias, a_log, d_skip,
           ssd_norm, w_pool_out, w_ssd_out, w_out, norm_ffn, w_router_group, b_router_group,
           w_router_expert, b_router_expert, w_gate_up, w_down, norm_final):
    bsz, seq, _ = x.shape
    lp = LEAD + N_META + seq
    assert lp % TM == 0 and seq % CHUNK == 0
    rows = bsz * lp

    meta = jnp.broadcast_to(meta_tokens.astype(x.dtype)[None], (bsz, N_META, D_MODEL))
    hp = jnp.concatenate([jnp.zeros((bsz, LEAD, D_MODEL), x.dtype), meta, x], axis=1)
    hp = hp.reshape(rows, D_MODEL)

    wi = w_in[0]
    o_z, o_xbc, o_dt, o_gp, o_gs = 1024, 3072, 6144, 6176, 7200
    w_main = jnp.concatenate([wi[:, o_z:o_xbc], wi[:, o_xbc:o_dt], wi[:, :o_z],
                              wi[:, o_gp:o_gs], wi[:, o_gs:]], axis=1).astype(BF16)
    w_dt = _pad_lanes(wi[:, o_dt:o_gp]).astype(BF16)
    gain = _row(norm_mix[0])
    proj = _in_proj(hp, gain, w_main, PROJ_TN)
    dt_raw = _in_proj(hp, gain, w_dt, LANES)

    yp = _pool_branch(proj, pool_w[0].astype(BF16), _row(pool_scale[0]),
                      w_pool_out[0].astype(BF16), lp)
    ys = _ssd_branch(proj, dt_raw, conv_w[0].astype(F32), _row(conv_b[0]),
                     _pad_lanes(_row(dt_bias[0])), _pad_lanes(_row(a_log[0])),
                     _row(jnp.repeat(d_skip[0], SSD_HEAD_DIM)), _row(ssd_norm[0]),
                     w_ssd_out[0].astype(BF16), bsz, lp)

    w_router = _pad_lanes(jnp.concatenate([w_router_group[0], w_router_expert[0]], axis=1).astype(F32))
    b_router = _pad_lanes(jnp.concatenate([_row(b_router_group[0]), _row(b_router_expert[0])], axis=1))
    h1, hn, route, counts = _merge_router(hp, yp, ys, w_out[0].astype(BF16), _row(norm_ffn[0]),
                                          w_router, b_router, lp)

    n_assign = 2 * bsz * (N_META + seq)
    nb = -(-n_assign // MOE_BLOCK) + N_EXPERTS
    cnt = counts[0, :N_EXPERTS].astype(I32)
    pcnt = (cnt + MOE_BLOCK - 1) // MOE_BLOCK * MOE_BLOCK
    pends = jnp.cumsum(pcnt)
    pstarts = pends - pcnt
    experts = route[:, 0:2].astype(I32)
    eids = jnp.arange(N_EXPERTS, dtype=I32)
    start_of = jnp.sum(jnp.where(experts[..., None] == eids, pstarts, 0), axis=-1)
    dest = start_of + route[:, 2:4].astype(I32)
    dest = dest.reshape(rows // CHUNK, CHUNK, 2).transpose(0, 2, 1)
    block_first_row = jnp.arange(nb, dtype=I32) * MOE_BLOCK
    block_e = jnp.minimum(jnp.sum((pends[None, :] <= block_first_row[:, None]).astype(I32), axis=1),
                          N_EXPERTS - 1)
    nused = (pends[-1:] // MOE_BLOCK).astype(I32)

    xs = _dispatch(dest, hn, jnp.zeros((nb * MOE_BLOCK, D_MODEL), F32), bsz, lp)
    yexp = _experts(block_e, nused, xs, w_gate_up[0], w_down[0])
    return _combine(dest, h1, route, _row(norm_final), yexp, bsz, lp, seq)
```

```python
import functools
import math

import jax
import jax.numpy as jnp
from jax import lax
from jax.experimental import pallas as pl
from jax.experimental.pallas import tpu as pltpu

F32 = jnp.float32
BF16 = jnp.bfloat16
I32 = jnp.int32

D_MODEL = 1024
N_META = 16
EPS = 1e-6
LOG2E = math.log2(math.e)
CHUNK = 128
LEAD = CHUNK - N_META
CPT = 5
TM = CPT * CHUNK
POOL_WINDOWS = (2, 4, 8, 16)
POOL_GROUP = D_MODEL // len(POOL_WINDOWS)
POOL_HALO = 16
SSD_INNER = 2 * D_MODEL
SSD_HEADS = 32
SSD_HEAD_DIM = 64
SSD_GROUPS = 4
SSD_STATE = 128
SSD_CONV = 4
CONV_HALO = 8
BC_WIDTH = 2 * SSD_GROUPS * SSD_STATE
GROUP_WIDTH = SSD_INNER // SSD_GROUPS
PAIRS_PER_GROUP = GROUP_WIDTH // 128
N_EXPERT_GROUPS = 8
EXPERTS_PER_GROUP = 8
N_EXPERTS = 64
D_EXPERT = 512
MOE_BLOCK = 256
LANES = 128
DMA_UNROLL = 8
STRIP = 512

VMEM_LIMIT = 48 * 1024 * 1024


def _cparams(*sem):
    return pltpu.CompilerParams(dimension_semantics=sem, vmem_limit_bytes=VMEM_LIMIT)


def _sigmoid(x):
    return 1.0 / (1.0 + jnp.exp(-x))


def _silu_of_half(h):
    return h + h * jnp.tanh(h)


def _normed(h_ref, g_ref):
    x = h_ref[...]
    return (x * lax.rsqrt(jnp.mean(x * x, axis=-1, keepdims=True) + EPS) * g_ref[...]).astype(BF16)


def _proj_silu_kernel(h_ref, g_ref, w_ref, o_ref):
    hn = _normed(h_ref, g_ref)
    for c0 in range(0, o_ref.shape[1], STRIP):
        half = jnp.dot(hn, w_ref[:, c0:c0 + STRIP], preferred_element_type=F32)
        o_ref[:, c0:c0 + STRIP] = _silu_of_half(half).astype(BF16)


def _proj_gate_kernel(h_ref, g_ref, w_ref, wdt_ref, o_ref, dt_ref):
    hn = _normed(h_ref, g_ref)
    for c0 in range(0, o_ref.shape[1], STRIP):
        half = jnp.dot(hn, w_ref[:, c0:c0 + STRIP], preferred_element_type=F32)
        o_ref[:, c0:c0 + STRIP] = (0.5 + 0.5 * jnp.tanh(half)).astype(BF16)
    dt_ref[...] = jnp.dot(hn, wdt_ref[...], preferred_element_type=F32)


def _proj_conv_kernel(h_ref, g_ref, w_ref, cw_ref, cb_ref, *refs, lp, n_conv, n_raw):
    if n_raw:
        o_ref, raw_ref, pbuf = refs
    else:
        (o_ref, pbuf), raw_ref = refs, None
    i = pl.program_id(0)

    @pl.when(i == 0)
    def _():
        pbuf[0:CONV_HALO, :] = jnp.zeros((CONV_HALO, n_conv), F32)

    hn = _normed(h_ref, g_ref)
    row = lax.rem(i * TM, lp) + lax.broadcasted_iota(I32, (CHUNK, 1), 0)
    lead_mask = row >= LEAD
    for c0 in range(0, n_conv, STRIP):
        cols = slice(c0, c0 + STRIP)
        pbuf[CONV_HALO:, cols] = jnp.dot(hn, w_ref[:, cols], preferred_element_type=F32)
        for r0 in range(0, TM, CHUNK):
            xa = pbuf[r0:r0 + CONV_HALO + CHUNK, cols]
            half = cb_ref[:, cols] + cw_ref[SSD_CONV - 1:SSD_CONV, cols] * xa[CONV_HALO:, :]
            for s in range(1, SSD_CONV):
                tap = cw_ref[SSD_CONV - 1 - s:SSD_CONV - s, cols]
                half = half + tap * pltpu.roll(xa, s, axis=0)[CONV_HALO:, :]
            act = _silu_of_half(half)
            if r0 == 0:
                act = jnp.where(lead_mask, act, 0.0)
            o_ref[r0:r0 + CHUNK, cols] = act.astype(BF16)
    pbuf[0:CONV_HALO, :] = pbuf[TM:TM + CONV_HALO, :]
    for c0 in range(0, n_raw, STRIP):
        raw_ref[:, c0:c0 + STRIP] = jnp.dot(hn, w_ref[:, n_conv + c0:n_conv + c0 + STRIP],
                                            preferred_element_type=F32)


def _proj_call(body, hp, gain, consts, outs, scratch=(), sem="parallel", name="proj"):
    rows = hp.shape[0]
    tile = lambda i: (i, 0)
    const = lambda i: (0, 0)
    return pl.pallas_call(
        body,
        grid=(rows // TM,),
        in_specs=[pl.BlockSpec((TM, D_MODEL), tile), pl.BlockSpec((1, D_MODEL), const)]
        + [pl.BlockSpec(c.shape, const) for c in consts],
        out_specs=[pl.BlockSpec((TM, n), tile) for n, _ in outs],
        out_shape=[jax.ShapeDtypeStruct((rows, n), dt) for n, dt in outs],
        scratch_shapes=list(scratch),
        compiler_params=_cparams(sem),
        name=name,
    )(hp, gain, *consts)


def _proj_conv_call(hp, gain, w, cw, cb, lp, n_conv, n_raw, name):
    outs = [(n_conv, BF16)] + ([(n_raw, F32)] if n_raw else [])
    return _proj_call(functools.partial(_proj_conv_kernel, lp=lp, n_conv=n_conv, n_raw=n_raw),
                      hp, gain, (w, cw, cb), outs,
                      scratch=[pltpu.VMEM((CONV_HALO + TM, n_conv), F32)], sem="arbitrary", name=name)


def _pool_kernel(u_ref, uprev_ref, gp_ref, pw_ref, ps_ref, wo_ref, o_ref, ubuf, *, lp):
    i = pl.program_id(0)
    ubuf[0:POOL_HALO, :] = uprev_ref[...]
    ubuf[POOL_HALO:POOL_HALO + TM, :] = u_ref[...]
    row = lax.rem(i * TM, lp) + lax.broadcasted_iota(I32, (TM, 1), 0)
    tpos = row - LEAD
    parts = []
    for k, w in enumerate(POOL_WINDOWS):
        lo = k * POOL_GROUP
        cur = ubuf[POOL_HALO:POOL_HALO + TM, lo:lo + POOL_GROUP]
        acc = cur
        for s in range(1, w):
            acc = acc + ubuf[POOL_HALO - s:POOL_HALO - s + TM, lo:lo + POOL_GROUP]
        cnt = jnp.clip(tpos + 1, 1, w).astype(F32)
        d = acc / cnt - cur
        parts.append(jnp.dot(d.astype(BF16), pw_ref[k], preferred_element_type=F32))
    pm = jnp.concatenate(parts, axis=-1) * ps_ref[...]
    y = jnp.dot(pm.astype(BF16), wo_ref[...], preferred_element_type=F32)
    o_ref[...] = (gp_ref[...].astype(F32) * y).astype(BF16)


def _pool_branch(u, gates, pool_w, pool_scale, w_pool_out, lp):
    rows = u.shape[0]
    halo_blocks = TM // POOL_HALO
    return pl.pallas_call(
        functools.partial(_pool_kernel, lp=lp),
        grid=(rows // TM,),
        in_specs=[pl.BlockSpec((TM, D_MODEL), lambda i: (i, 0)),
                  pl.BlockSpec((POOL_HALO, D_MODEL), lambda i: (jnp.maximum(i * halo_blocks - 1, 0), 0)),
                  pl.BlockSpec((TM, D_MODEL), lambda i: (i, 0)),
                  pl.BlockSpec((len(POOL_WINDOWS), POOL_GROUP, POOL_GROUP), lambda i: (0, 0, 0)),
                  pl.BlockSpec((1, D_MODEL), lambda i: (0, 0)),
                  pl.BlockSpec((D_MODEL, D_MODEL), lambda i: (0, 0))],
        out_specs=pl.BlockSpec((TM, D_MODEL), lambda i: (i, 0)),
        out_shape=jax.ShapeDtypeStruct((rows, D_MODEL), BF16),
        scratch_shapes=[pltpu.VMEM((POOL_HALO + TM, D_MODEL), F32)],
        compiler_params=_cparams("parallel"),
        name="pool_branch",
    )(u, u, gates, pool_w, pool_scale, w_pool_out)


def _pair_lanes(cols, h0, left):
    return jnp.where(left, cols[:, h0:h0 + 1], cols[:, h0 + 1:h0 + 2])


def _ssd_kernel(z_ref, x_ref, bc_ref, dt_ref, gs_ref, dtb_ref, alog_ref, dsk_ref, nrm_ref, wo_ref,
                o_ref, y_ref, st_ref, yn_ref):
    c = pl.program_id(1)

    @pl.when(c == 0)
    def _():
        st_ref[...] = jnp.zeros_like(st_ref)

    rowi = lax.broadcasted_iota(I32, (CHUNK, 1), 0)
    coli = lax.broadcasted_iota(I32, (1, CHUNK), 1)
    causal = rowi >= coli
    left = coli < SSD_HEAD_DIM

    dtr = dt_ref[...] + dtb_ref[...]
    dt = jnp.maximum(dtr, 0.0) + jnp.log(1.0 + jnp.exp(-jnp.abs(dtr)))
    dt = jnp.where(rowi >= jnp.where(c == 0, LEAD, 0), dt, 0.0)
    adt2 = dt * (-LOG2E * jnp.exp(alog_ref[...]))
    a2 = jnp.dot(causal.astype(F32), adt2, precision=lax.Precision.HIGHEST,
                 preferred_element_type=F32)
    a2_t = a2.T

    for g in range(SSD_GROUPS):
        b_g = bc_ref[:, g * SSD_STATE:(g + 1) * SSD_STATE]
        c_g = bc_ref[:, (SSD_GROUPS + g) * SSD_STATE:(SSD_GROUPS + g + 1) * SSD_STATE]
        cb = lax.dot_general(c_g, b_g, (((1,), (1,)), ((), ())), preferred_element_type=F32)
        b_gt = b_g.T
        st_g = st_ref[g]
        y_off = jnp.dot(c_g, st_g.astype(BF16), preferred_element_type=F32)
        xw_parts, dec_parts = [], []
        for q in range(PAIRS_PER_GROUP):
            pair = g * PAIRS_PER_GROUP + q
            h0 = 2 * pair
            lanes = slice(pair * LANES, (pair + 1) * LANES)
            xq = x_ref[:, lanes].astype(F32)
            a_cols = [jnp.broadcast_to(a2[:, h:h + 1], (CHUNK, CHUNK)) for h in (h0, h0 + 1)]
            a_pair = jnp.where(left, a_cols[0], a_cols[1])
            dt_pair = _pair_lanes(dt, h0, left)
            a_pair_last = a_pair[CHUNK - 1:CHUNK, :]
            xdt = (xq * dt_pair).astype(BF16)
            heads = []
            for j, h in enumerate((h0, h0 + 1)):
                seg = jnp.where(causal, a_cols[j] - a2_t[h:h + 1, :], -jnp.inf)
                m = (cb * jnp.exp2(seg)).astype(BF16)
                heads.append(jnp.dot(m, xdt, preferred_element_type=F32))
            y_ref[:, lanes] = (jnp.where(left, heads[0], heads[1])
                               + y_off[:, q * LANES:(q + 1) * LANES] * jnp.exp2(a_pair)
                               + dsk_ref[:, lanes] * xq)
            xw_parts.append((xq * (dt_pair * jnp.exp2(a_pair_last - a_pair))).astype(BF16))
            dec_parts.append(jnp.exp2(a_pair_last))
        xw = jnp.concatenate(xw_parts, axis=-1)
        dec = jnp.concatenate(dec_parts, axis=-1)
        st_ref[g] = st_g * dec + jnp.dot(b_gt, xw, preferred_element_type=F32)

    y = y_ref[...] * z_ref[...].astype(F32)
    outs = []
    for g in range(SSD_GROUPS):
        yg = y[:, g * GROUP_WIDTH:(g + 1) * GROUP_WIDTH]
        outs.append(yg * lax.rsqrt(jnp.mean(yg * yg, axis=-1, keepdims=True) + EPS))
    yn = jnp.concatenate(outs, axis=-1) * nrm_ref[...]
    slot = lax.rem(c, CPT)
    yn_ref[pl.ds(pl.multiple_of(slot * CHUNK, CHUNK), CHUNK), :] = yn.astype(BF16)

    @pl.when(slot == CPT - 1)
    def _():
        yo = jnp.dot(yn_ref[...], wo_ref[...], preferred_element_type=F32)
        o_ref[...] = (gs_ref[...].astype(F32) * yo).astype(BF16)


def _ssd_branch(z_act, x_act, bc_act, dt_raw, gates, dt_bias, a_log, d_skip, ssd_norm, w_ssd_out, bsz, lp):
    rows = z_act.shape[0]
    nc = lp // CHUNK
    tiles = nc // CPT
    chunk = lambda b, c: (b * nc + c, 0)
    const = lambda b, c: (0, 0)
    return pl.pallas_call(
        _ssd_kernel,
        grid=(bsz, nc),
        in_specs=[pl.BlockSpec((CHUNK, SSD_INNER), chunk),
                  pl.BlockSpec((CHUNK, SSD_INNER), chunk),
                  pl.BlockSpec((CHUNK, BC_WIDTH), chunk),
                  pl.BlockSpec((CHUNK, LANES), chunk),
                  pl.BlockSpec((TM, D_MODEL), lambda b, c: (b * tiles + c // CPT, 1)),
                  pl.BlockSpec((1, LANES), const),
                  pl.BlockSpec((1, LANES), const),
                  pl.BlockSpec((1, SSD_INNER), const),
                  pl.BlockSpec((1, SSD_INNER), const),
                  pl.BlockSpec((SSD_INNER, D_MODEL), const)],
        out_specs=pl.BlockSpec((TM, D_MODEL), lambda b, c: (b * tiles + c // CPT, 0)),
        out_shape=jax.ShapeDtypeStruct((rows, D_MODEL), BF16),
        scratch_shapes=[pltpu.VMEM((CHUNK, SSD_INNER), F32),
                        pltpu.VMEM((SSD_GROUPS, SSD_STATE, GROUP_WIDTH), F32),
                        pltpu.VMEM((TM, SSD_INNER), BF16)],
        compiler_params=_cparams("parallel", "arbitrary"),
        name="ssd_branch",
    )(z_act, x_act, bc_act, dt_raw, gates, dt_bias, a_log, d_skip, ssd_norm, w_ssd_out)


def _first_index_of_max(vals, lane):
    m = jnp.max(vals, axis=-1, keepdims=True)
    idx = jnp.min(jnp.where(vals == m, lane, LANES), axis=-1, keepdims=True)
    return m, idx


def _merge_kernel(h_ref, yp_ref, ys_ref, wo_ref, nf_ref, wr_ref, br_ref,
                  h1_ref, hn_ref, route_ref, cnt_ref, carry, *, lp):
    i = pl.program_id(0)

    @pl.when(i == 0)
    def _():
        carry[...] = jnp.zeros_like(carry)

    merged = yp_ref[...].astype(F32) + ys_ref[...].astype(F32)
    h1 = h_ref[...] + jnp.dot(merged.astype(BF16), wo_ref[...], preferred_element_type=F32)
    h1_ref[...] = h1
    hn = h1 * lax.rsqrt(jnp.mean(h1 * h1, axis=-1, keepdims=True) + EPS) * nf_ref[...]
    hn_ref[...] = hn

    logits = jnp.dot(hn, wr_ref[...], precision=lax.Precision.HIGHEST,
                     preferred_element_type=F32) + br_ref[...]
    lane = lax.broadcasted_iota(I32, (1, LANES), 1)
    neg = -jnp.inf
    gl = jnp.where(lane < N_EXPERT_GROUPS, logits, neg)
    gmax, gidx = _first_index_of_max(gl, lane)
    g_p = 1.0 / jnp.sum(jnp.exp(gl - gmax), axis=-1, keepdims=True)
    elo = N_EXPERT_GROUPS + gidx * EXPERTS_PER_GROUP
    el = jnp.where((lane >= elo) & (lane < elo + EXPERTS_PER_GROUP), logits, neg)
    m1, i1 = _first_index_of_max(el, lane)
    m2, i2 = _first_index_of_max(jnp.where(lane == i1, neg, el), lane)
    t = jnp.exp(m2 - m1)
    w1 = g_p / (1.0 + t)
    w2 = g_p * t / (1.0 + t)
    e1 = i1 - N_EXPERT_GROUPS
    e2 = i2 - N_EXPERT_GROUPS

    rowi = lax.broadcasted_iota(I32, (TM, 1), 0)
    valid = (lax.rem(i * TM, lp) + rowi) >= LEAD
    o1 = jnp.where((lane == e1) & valid, 1.0, 0.0)
    o2 = jnp.where((lane == e2) & valid, 1.0, 0.0)
    both = o1 + o2
    earlier = (rowi > lax.broadcasted_iota(I32, (1, TM), 1)).astype(BF16)
    before = jnp.dot(earlier, both.astype(BF16), preferred_element_type=F32) + carry[...]
    r1 = jnp.sum(o1 * before, axis=-1, keepdims=True)
    r2 = jnp.sum(o2 * before, axis=-1, keepdims=True)
    carry[...] = carry[...] + jnp.sum(both, axis=0, keepdims=True)
    cnt_ref[...] = carry[...]

    route = jnp.where(lane == 0, e1.astype(F32), 0.0)
    route = jnp.where(lane == 1, e2.astype(F32), route)
    route = jnp.where(lane == 2, r1, route)
    route = jnp.where(lane == 3, r2, route)
    route = jnp.where(lane == 4, w1, route)
    route = jnp.where(lane == 5, w2, route)
    route_ref[...] = route


def _merge_router(hp, yp, ys, w_out, norm_ffn, w_router, b_router, lp):
    rows = hp.shape[0]
    tile = lambda i: (i, 0)
    const = lambda i: (0, 0)
    return pl.pallas_call(
        functools.partial(_merge_kernel, lp=lp),
        grid=(rows // TM,),
        in_specs=[pl.BlockSpec((TM, D_MODEL), tile),
                  pl.BlockSpec((TM, D_MODEL), tile),
                  pl.BlockSpec((TM, D_MODEL), tile),
                  pl.BlockSpec((D_MODEL, D_MODEL), const),
                  pl.BlockSpec((1, D_MODEL), const),
                  pl.BlockSpec((D_MODEL, LANES), const),
                  pl.BlockSpec((1, LANES), const)],
        out_specs=[pl.BlockSpec((TM, D_MODEL), tile),
                   pl.BlockSpec((TM, D_MODEL), tile),
                   pl.BlockSpec((TM, LANES), tile),
                   pl.BlockSpec((1, LANES), const)],
        out_shape=[jax.ShapeDtypeStruct((rows, D_MODEL), F32),
                   jax.ShapeDtypeStruct((rows, D_MODEL), F32),
                   jax.ShapeDtypeStruct((rows, LANES), F32),
                   jax.ShapeDtypeStruct((1, LANES), F32)],
        scratch_shapes=[pltpu.VMEM((1, LANES), F32)],
        compiler_params=_cparams("arbitrary"),
        name="merge_router",
    )(hp, yp, ys, w_out, norm_ffn, w_router, b_router)


def _row_copy(src, src_row, dst, dst_row, sem):
    return pltpu.make_async_copy(src.at[pl.ds(src_row, 1)], dst.at[pl.ds(dst_row, 1)], sem)


def _wait_rows(buf, nrows, sem):
    view = buf.at[pl.ds(0, nrows)]
    pltpu.make_async_copy(view, view, sem).wait()


def _dispatch_kernel(dest_ref, hn_ref, xs_in, xs_out, sem):
    del xs_in
    first = pl.program_id(1) == 0

    def issue(blk, lo):
        def body(r, carry):
            for k in range(2):
                _row_copy(hn_ref, blk * CHUNK + r, xs_out, dest_ref[blk, k, r], sem).start()
            return carry
        lax.fori_loop(lo, CHUNK, body, 0, unroll=DMA_UNROLL)

    @pl.when(first)
    def _():
        issue(0, LEAD)

    @pl.when(jnp.logical_not(first))
    def _():
        issue(0, 0)

    for blk in range(1, CPT):
        issue(blk, 0)

    @pl.when(first)
    def _():
        for _k in range(2):
            _wait_rows(hn_ref, TM - LEAD, sem)

    @pl.when(jnp.logical_not(first))
    def _():
        for _k in range(2):
            _wait_rows(hn_ref, TM, sem)


def _dispatch(dest, hn, xs_zero, bsz, lp):
    tiles = lp // TM
    return pl.pallas_call(
        _dispatch_kernel,
        grid=(bsz, tiles),
        in_specs=[pl.BlockSpec((CPT, 2, CHUNK), lambda b, c: (b * tiles + c, 0, 0),
                               memory_space=pltpu.SMEM),
                  pl.BlockSpec((TM, D_MODEL), lambda b, c: (b * tiles + c, 0)),
                  pl.BlockSpec(memory_space=pl.ANY)],
        out_specs=pl.BlockSpec(memory_space=pl.ANY),
        out_shape=jax.ShapeDtypeStruct(xs_zero.shape, xs_zero.dtype),
        scratch_shapes=[pltpu.SemaphoreType.DMA(())],
        input_output_aliases={2: 0},
        compiler_params=_cparams("arbitrary", "arbitrary"),
        name="moe_dispatch",
    )(dest, hn, xs_zero)


def _expert_kernel(be_ref, nused_ref, x_ref, wgu_ref, wdn_ref, y_ref, wgu16, wdn16):
    i = pl.program_id(0)

    @pl.when(i < nused_ref[0])
    def _():
        @pl.when((i == 0) | (be_ref[i] != be_ref[jnp.maximum(i - 1, 0)]))
        def _():
            wgu16[...] = wgu_ref[0].astype(BF16)
            wdn16[...] = wdn_ref[0].astype(BF16)

        gu = jnp.dot(x_ref[...].astype(BF16), wgu16[...], preferred_element_type=F32)
        gate, up = gu[:, :D_EXPERT], gu[:, D_EXPERT:]
        act = gate * _sigmoid(gate) * up
        y_ref[...] = jnp.dot(act.astype(BF16), wdn16[...], preferred_element_type=F32)

    @pl.when(i >= nused_ref[0])
    def _():
        y_ref[...] = jnp.zeros_like(y_ref)


def _experts(block_e, nused, xs, w_gu, w_dn):
    nb = xs.shape[0] // MOE_BLOCK
    last_used = lambda i, nu: jnp.maximum(jnp.minimum(i, nu[0] - 1), 0)
    blk = lambda i, be, nu: (last_used(i, nu), 0)
    wsel = lambda i, be, nu: (be[last_used(i, nu)], 0, 0)
    oblk = lambda i, be, nu: (i, 0)
    return pl.pallas_call(
        _expert_kernel,
        grid_spec=pltpu.PrefetchScalarGridSpec(
            num_scalar_prefetch=2,
            grid=(nb,),
            in_specs=[pl.BlockSpec((MOE_BLOCK, D_MODEL), blk),
                      pl.BlockSpec((1, D_MODEL, 2 * D_EXPERT), wsel),
                      pl.BlockSpec((1, D_EXPERT, D_MODEL), wsel)],
            out_specs=pl.BlockSpec((MOE_BLOCK, D_MODEL), oblk),
            scratch_shapes=[pltpu.VMEM((D_MODEL, 2 * D_EXPERT), BF16),
                            pltpu.VMEM((D_EXPERT, D_MODEL), BF16)]),
        out_shape=jax.ShapeDtypeStruct(xs.shape, F32),
        compiler_params=_cparams("arbitrary"),
        name="moe_experts",
    )(block_e, nused, xs, w_gu, w_dn)


def _combine_kernel(da_ref, db_ref, dn_ref, h1a_ref, h1b_ref, rta_ref, rtb_ref, nf_ref, y_hbm,
                    o_ref, ya, yb, sem):
    step = pl.program_id(0) * pl.num_programs(1) + pl.program_id(1)
    nsteps = pl.num_programs(0) * pl.num_programs(1)

    def issue(d_ref, slot):
        def body(r, carry):
            _row_copy(y_hbm, d_ref[0, 0, r], ya.at[slot], r, sem.at[slot, 0]).start()
            _row_copy(y_hbm, d_ref[0, 1, r], yb.at[slot], r, sem.at[slot, 1]).start()
            return carry
        lax.fori_loop(0, CHUNK, body, 0, unroll=DMA_UNROLL)

    def finish(slot, h1_ref, rt_ref):
        _wait_rows(ya.at[slot], CHUNK, sem.at[slot, 0])
        _wait_rows(yb.at[slot], CHUNK, sem.at[slot, 1])
        rt = rt_ref[...]
        h2 = h1_ref[...] + rt[:, 4:5] * ya[slot] + rt[:, 5:6] * yb[slot]
        o_ref[0, slot * CHUNK:(slot + 1) * CHUNK, :] = (
            h2 * lax.rsqrt(jnp.mean(h2 * h2, axis=-1, keepdims=True) + EPS) * nf_ref[...])

    @pl.when(step == 0)
    def _():
        issue(da_ref, 0)

    issue(db_ref, 1)
    finish(0, h1a_ref, rta_ref)

    @pl.when(step + 1 < nsteps)
    def _():
        issue(dn_ref, 0)

    finish(1, h1b_ref, rtb_ref)


def _combine(dest, h1, route, norm_final, ys, bsz, lp, seq):
    nc = lp // CHUNK
    half = seq // (2 * CHUNK)
    first = lambda b, j: b * nc + 1 + 2 * j
    second = lambda b, j: b * nc + 2 + 2 * j

    def upcoming(b, j):
        nxt = jnp.minimum(b * half + j + 1, bsz * half - 1)
        return first(nxt // half, nxt % half)

    smem3 = lambda f: pl.BlockSpec((1, 2, CHUNK), lambda b, j: (f(b, j), 0, 0), memory_space=pltpu.SMEM)
    rows2 = lambda f, width: pl.BlockSpec((CHUNK, width), lambda b, j: (f(b, j), 0))
    return pl.pallas_call(
        _combine_kernel,
        grid=(bsz, half),
        in_specs=[smem3(first), smem3(second), smem3(upcoming),
                  rows2(first, D_MODEL), rows2(second, D_MODEL),
                  rows2(first, LANES), rows2(second, LANES),
                  pl.BlockSpec((1, D_MODEL), lambda b, j: (0, 0)),
                  pl.BlockSpec(memory_space=pl.ANY)],
        out_specs=pl.BlockSpec((1, 2 * CHUNK, D_MODEL), lambda b, j: (b, j, 0)),
        out_shape=jax.ShapeDtypeStruct((bsz, seq, D_MODEL), F32),
        scratch_shapes=[pltpu.VMEM((2, CHUNK, D_MODEL), F32),
                        pltpu.VMEM((2, CHUNK, D_MODEL), F32),
                        pltpu.SemaphoreType.DMA((2, 2))],
        compiler_params=_cparams("arbitrary", "arbitrary"),
        name="moe_combine",
    )(dest, dest, dest, h1, h1, route, route, norm_final, ys)


def _row(v):
    return v.reshape(1, -1).astype(F32)


def _pad_lanes(v):
    return jnp.pad(v, ((0, 0), (0, LANES - v.shape[1])))


def kernel(x, meta_tokens, norm_mix, w_in, pool_w, pool_scale, conv_w, conv_b, dt_bias, a_log, d_skip,
           ssd_norm, w_pool_out, w_ssd_out, w_out, norm_ffn, w_router_group, b_router_group,
           w_router_expert, b_router_expert, w_gate_up, w_down, norm_final):
    bsz, seq, _ = x.shape
    lp = LEAD + N_META + seq
    assert lp % TM == 0 and seq % (2 * CHUNK) == 0
    rows = bsz * lp

    meta = jnp.broadcast_to(meta_tokens.astype(x.dtype)[None], (bsz, N_META, D_MODEL))
    hp = jnp.concatenate([jnp.zeros((bsz, LEAD, D_MODEL), x.dtype), meta, x], axis=1)
    hp = hp.reshape(rows, D_MODEL)

    wi = w_in[0]
    o_z, o_x, o_bc, o_dt, o_gp = 1024, 3072, 5120, 6144, 6176
    gain = _row(norm_mix[0])
    half_w = lambda cols: (0.5 * cols).astype(BF16)
    cw_half = 0.5 * conv_w[0].astype(F32)
    cb_half = 0.5 * _row(conv_b[0])
    (z_act,) = _proj_call(_proj_silu_kernel, hp, gain, (half_w(wi[:, o_z:o_x]),),
                          [(SSD_INNER, BF16)], name="proj_z")
    (x_act,) = _proj_conv_call(hp, gain, wi[:, o_x:o_bc].astype(BF16), cw_half[:, :SSD_INNER],
                               cb_half[:, :SSD_INNER], lp, SSD_INNER, 0, "proj_x")
    w_bcu = jnp.concatenate([wi[:, o_bc:o_dt], wi[:, :o_z]], axis=1).astype(BF16)
    bc_act, u = _proj_conv_call(hp, gain, w_bcu, cw_half[:, SSD_INNER:], cb_half[:, SSD_INNER:],
                                lp, BC_WIDTH, D_MODEL, "proj_bcu")
    gates, dt_raw = _proj_call(_proj_gate_kernel, hp, gain,
                               (half_w(wi[:, o_gp:]), _pad_lanes(wi[:, o_dt:o_gp]).astype(BF16)),
                               [(2 * D_MODEL, BF16), (LANES, F32)], name="proj_gates")

    yp = _pool_branch(u, gates, pool_w[0].astype(BF16), _row(pool_scale[0]),
                      w_pool_out[0].astype(BF16), lp)
    ys = _ssd_branch(z_act, x_act, bc_act, dt_raw, gates,
                     _pad_lanes(_row(dt_bias[0])), _pad_lanes(_row(a_log[0])),
                     _row(jnp.repeat(d_skip[0], SSD_HEAD_DIM)), _row(ssd_norm[0]),
                     w_ssd_out[0].astype(BF16), bsz, lp)

    w_router = _pad_lanes(jnp.concatenate([w_router_group[0], w_router_expert[0]], axis=1).astype(F32))
    b_router = _pad_lanes(jnp.concatenate([_row(b_router_group[0]), _row(b_router_expert[0])], axis=1))
    h1, hn, route, counts = _merge_router(hp, yp, ys, w_out[0].astype(BF16), _row(norm_ffn[0]),
                                          w_router, b_router, lp)

    n_assign = 2 * bsz * (N_META + seq)
    nb = -(-n_assign // MOE_BLOCK) + N_EXPERTS
    cnt = counts[0, :N_EXPERTS].astype(I32)
    pcnt = (cnt + MOE_BLOCK - 1) // MOE_BLOCK * MOE_BLOCK
    pends = jnp.cumsum(pcnt)
    pstarts = pends - pcnt
    experts = route[:, 0:2].astype(I32)
    eids = jnp.arange(N_EXPERTS, dtype=I32)
    start_of = jnp.sum(jnp.where(experts[..., None] == eids, pstarts, 0), axis=-1)
    dest = start_of + route[:, 2:4].astype(I32)
    dest = dest.reshape(rows // CHUNK, CHUNK, 2).transpose(0, 2, 1)
    block_first_row = jnp.arange(nb, dtype=I32) * MOE_BLOCK
    block_e = jnp.minimum(jnp.sum((pends[None, :] <= block_first_row[:, None]).astype(I32), axis=1),
                          N_EXPERTS - 1)
    nused = (pends[-1:] // MOE_BLOCK).astype(I32)

    xs = _dispatch(dest, hn, jnp.zeros((nb * MOE_BLOCK, D_MODEL), F32), bsz, lp)
    yexp = _experts(block_e, nused, xs, w_gate_up[0], w_down[0])
    return _combine(dest, h1, route, _row(norm_final), yexp, bsz, lp, seq)
```

```python
import functools
import math

import jax
import jax.numpy as jnp
from jax import lax
from jax.experimental import pallas as pl
from jax.experimental.pallas import tpu as pltpu

F32 = jnp.float32
BF16 = jnp.bfloat16
I32 = jnp.int32

D_MODEL = 1024
N_META = 16
EPS = 1e-6
LOG2E = math.log2(math.e)
CHUNK = 128
LEAD = CHUNK - N_META
CPT = 5
TM = CPT * CHUNK
POOL_WINDOWS = (2, 4, 8, 16)
POOL_GROUP = D_MODEL // len(POOL_WINDOWS)
POOL_HALO = 16
SSD_INNER = 2 * D_MODEL
SSD_HEADS = 32
SSD_HEAD_DIM = 64
SSD_GROUPS = 4
SSD_STATE = 128
SSD_CONV = 4
CONV_HALO = 8
BC_WIDTH = 2 * SSD_GROUPS * SSD_STATE
GROUP_WIDTH = SSD_INNER // SSD_GROUPS
PAIRS_PER_GROUP = GROUP_WIDTH // 128
N_EXPERT_GROUPS = 8
EXPERTS_PER_GROUP = 8
N_EXPERTS = 64
D_EXPERT = 512
MOE_BLOCK = 256
LANES = 128
DMA_UNROLL = 8
ZERO_BLOCKS = 2 * N_EXPERTS
STRIP = 512

VMEM_LIMIT = 48 * 1024 * 1024


def _cparams(*sem):
    return pltpu.CompilerParams(dimension_semantics=sem, vmem_limit_bytes=VMEM_LIMIT)


def _sigmoid(x):
    return 1.0 / (1.0 + jnp.exp(-x))


def _silu_of_half(h):
    return h + h * jnp.tanh(h)


def _normed(h_ref, g_ref):
    x = h_ref[...]
    return (x * lax.rsqrt(jnp.mean(x * x, axis=-1, keepdims=True) + EPS) * g_ref[...]).astype(BF16)


def _proj_silu_kernel(h_ref, g_ref, w_ref, o_ref):
    hn = _normed(h_ref, g_ref)
    for c0 in range(0, o_ref.shape[1], STRIP):
        half = jnp.dot(hn, w_ref[:, c0:c0 + STRIP], preferred_element_type=F32)
        o_ref[:, c0:c0 + STRIP] = _silu_of_half(half).astype(BF16)


def _proj_gate_kernel(h_ref, g_ref, w_ref, wdt_ref, o_ref, dt_ref):
    hn = _normed(h_ref, g_ref)
    for c0 in range(0, o_ref.shape[1], STRIP):
        half = jnp.dot(hn, w_ref[:, c0:c0 + STRIP], preferred_element_type=F32)
        o_ref[:, c0:c0 + STRIP] = (0.5 + 0.5 * jnp.tanh(half)).astype(BF16)
    dt_ref[...] = jnp.dot(hn, wdt_ref[...], preferred_element_type=F32)


def _proj_conv_kernel(h_ref, g_ref, w_ref, cw_ref, cb_ref, *refs, lp, n_conv, n_raw):
    if n_raw:
        o_ref, raw_ref, pbuf = refs
    else:
        (o_ref, pbuf), raw_ref = refs, None
    i = pl.program_id(0)

    @pl.when(i == 0)
    def _():
        pbuf[0:CONV_HALO, :] = jnp.zeros((CONV_HALO, n_conv), F32)

    hn = _normed(h_ref, g_ref)
    row = lax.rem(i * TM, lp) + lax.broadcasted_iota(I32, (CHUNK, 1), 0)
    lead_mask = row >= LEAD
    for c0 in range(0, n_conv, STRIP):
        cols = slice(c0, c0 + STRIP)
        pbuf[CONV_HALO:, cols] = jnp.dot(hn, w_ref[:, cols], preferred_element_type=F32)
        for r0 in range(0, TM, CHUNK):
            xa = pbuf[r0:r0 + CONV_HALO + CHUNK, cols]
            half = cb_ref[:, cols] + cw_ref[SSD_CONV - 1:SSD_CONV, cols] * xa[CONV_HALO:, :]
            for s in range(1, SSD_CONV):
                tap = cw_ref[SSD_CONV - 1 - s:SSD_CONV - s, cols]
                half = half + tap * pltpu.roll(xa, s, axis=0)[CONV_HALO:, :]
            act = _silu_of_half(half)
            if r0 == 0:
                act = jnp.where(lead_mask, act, 0.0)
            o_ref[r0:r0 + CHUNK, cols] = act.astype(BF16)
    pbuf[0:CONV_HALO, :] = pbuf[TM:TM + CONV_HALO, :]
    for c0 in range(0, n_raw, STRIP):
        raw_ref[:, c0:c0 + STRIP] = jnp.dot(hn, w_ref[:, n_conv + c0:n_conv + c0 + STRIP],
                                            preferred_element_type=F32)


def _proj_call(body, hp, gain, consts, outs, scratch=(), sem="parallel", name="proj"):
    rows = hp.shape[0]
    tile = lambda i: (i, 0)
    const = lambda i: (0, 0)
    return pl.pallas_call(
        body,
        grid=(rows // TM,),
        in_specs=[pl.BlockSpec((TM, D_MODEL), tile), pl.BlockSpec((1, D_MODEL), const)]
        + [pl.BlockSpec(c.shape, const) for c in consts],
        out_specs=[pl.BlockSpec((TM, n), tile) for n, _ in outs],
        out_shape=[jax.ShapeDtypeStruct((rows, n), dt) for n, dt in outs],
        scratch_shapes=list(scratch),
        compiler_params=_cparams(sem),
        name=name,
    )(hp, gain, *consts)


def _proj_conv_call(hp, gain, w, cw, cb, lp, n_conv, n_raw, name):
    outs = [(n_conv, BF16)] + ([(n_raw, F32)] if n_raw else [])
    return _proj_call(functools.partial(_proj_conv_kernel, lp=lp, n_conv=n_conv, n_raw=n_raw),
                      hp, gain, (w, cw, cb), outs,
                      scratch=[pltpu.VMEM((CONV_HALO + TM, n_conv), F32)], sem="arbitrary", name=name)


def _pool_kernel(u_ref, uprev_ref, gp_ref, pw_ref, ps_ref, wo_ref, o_ref, ubuf, *, lp):
    i = pl.program_id(0)
    ubuf[0:POOL_HALO, :] = uprev_ref[...]
    ubuf[POOL_HALO:POOL_HALO + TM, :] = u_ref[...]
    row = lax.rem(i * TM, lp) + lax.broadcasted_iota(I32, (TM, 1), 0)
    tpos = row - LEAD
    parts = []
    for k, w in enumerate(POOL_WINDOWS):
        lo = k * POOL_GROUP
        cur = ubuf[POOL_HALO:POOL_HALO + TM, lo:lo + POOL_GROUP]
        acc = cur
        for s in range(1, w):
            acc = acc + ubuf[POOL_HALO - s:POOL_HALO - s + TM, lo:lo + POOL_GROUP]
        cnt = jnp.clip(tpos + 1, 1, w).astype(F32)
        d = acc / cnt - cur
        parts.append(jnp.dot(d.astype(BF16), pw_ref[k], preferred_element_type=F32))
    pm = jnp.concatenate(parts, axis=-1) * ps_ref[...]
    y = jnp.dot(pm.astype(BF16), wo_ref[...], preferred_element_type=F32)
    o_ref[...] = (gp_ref[...].astype(F32) * y).astype(BF16)


def _pool_branch(u, gates, pool_w, pool_scale, w_pool_out, lp):
    rows = u.shape[0]
    halo_blocks = TM // POOL_HALO
    return pl.pallas_call(
        functools.partial(_pool_kernel, lp=lp),
        grid=(rows // TM,),
        in_specs=[pl.BlockSpec((TM, D_MODEL), lambda i: (i, 0)),
                  pl.BlockSpec((POOL_HALO, D_MODEL), lambda i: (jnp.maximum(i * halo_blocks - 1, 0), 0)),
                  pl.BlockSpec((TM, D_MODEL), lambda i: (i, 0)),
                  pl.BlockSpec((len(POOL_WINDOWS), POOL_GROUP, POOL_GROUP), lambda i: (0, 0, 0)),
                  pl.BlockSpec((1, D_MODEL), lambda i: (0, 0)),
                  pl.BlockSpec((D_MODEL, D_MODEL), lambda i: (0, 0))],
        out_specs=pl.BlockSpec((TM, D_MODEL), lambda i: (i, 0)),
        out_shape=jax.ShapeDtypeStruct((rows, D_MODEL), BF16),
        scratch_shapes=[pltpu.VMEM((POOL_HALO + TM, D_MODEL), F32)],
        compiler_params=_cparams("parallel"),
        name="pool_branch",
    )(u, u, gates, pool_w, pool_scale, w_pool_out)


def _pair_lanes(cols, h0, left):
    return jnp.where(left, cols[:, h0:h0 + 1], cols[:, h0 + 1:h0 + 2])


def _ssd_kernel(z_ref, x_ref, bc_ref, dt_ref, gs_ref, dtb_ref, alog_ref, dsk_ref, nrm_ref, wo_ref,
                o_ref, y_ref, st_ref, yn_ref):
    c = pl.program_id(1)

    @pl.when(c == 0)
    def _():
        st_ref[...] = jnp.zeros_like(st_ref)

    rowi = lax.broadcasted_iota(I32, (CHUNK, 1), 0)
    coli = lax.broadcasted_iota(I32, (1, CHUNK), 1)
    causal = rowi >= coli
    left = coli < SSD_HEAD_DIM

    dtr = dt_ref[...] + dtb_ref[...]
    dt = jnp.maximum(dtr, 0.0) + jnp.log(1.0 + jnp.exp(-jnp.abs(dtr)))
    dt = jnp.where(rowi >= jnp.where(c == 0, LEAD, 0), dt, 0.0)
    adt2 = dt * (-LOG2E * jnp.exp(alog_ref[...]))
    a2 = jnp.dot(causal.astype(F32), adt2, precision=lax.Precision.HIGHEST,
                 preferred_element_type=F32)
    a2_t = a2.T

    for g in range(SSD_GROUPS):
        b_g = bc_ref[:, g * SSD_STATE:(g + 1) * SSD_STATE]
        c_g = bc_ref[:, (SSD_GROUPS + g) * SSD_STATE:(SSD_GROUPS + g + 1) * SSD_STATE]
        cb = lax.dot_general(c_g, b_g, (((1,), (1,)), ((), ())), preferred_element_type=F32)
        b_gt = b_g.T
        st_g = st_ref[g]
        y_off = jnp.dot(c_g, st_g.astype(BF16), preferred_element_type=F32)
        xw_parts, dec_parts = [], []
        for q in range(PAIRS_PER_GROUP):
            pair = g * PAIRS_PER_GROUP + q
            h0 = 2 * pair
            lanes = slice(pair * LANES, (pair + 1) * LANES)
            xq = x_ref[:, lanes].astype(F32)
            a_cols = [jnp.broadcast_to(a2[:, h:h + 1], (CHUNK, CHUNK)) for h in (h0, h0 + 1)]
            a_pair = jnp.where(left, a_cols[0], a_cols[1])
            dt_pair = _pair_lanes(dt, h0, left)
            a_pair_last = a_pair[CHUNK - 1:CHUNK, :]
            xdt = (xq * dt_pair).astype(BF16)
            heads = []
            for j, h in enumerate((h0, h0 + 1)):
                seg = jnp.where(causal, a_cols[j] - a2_t[h:h + 1, :], -jnp.inf)
                m = (cb * jnp.exp2(seg)).astype(BF16)
                heads.append(jnp.dot(m, xdt, preferred_element_type=F32))
            y_ref[:, lanes] = (jnp.where(left, heads[0], heads[1])
                               + y_off[:, q * LANES:(q + 1) * LANES] * jnp.exp2(a_pair)
                               + dsk_ref[:, lanes] * xq)
            xw_parts.append((xq * (dt_pair * jnp.exp2(a_pair_last - a_pair))).astype(BF16))
            dec_parts.append(jnp.exp2(a_pair_last))
        xw = jnp.concatenate(xw_parts, axis=-1)
        dec = jnp.concatenate(dec_parts, axis=-1)
        st_ref[g] = st_g * dec + jnp.dot(b_gt, xw, preferred_element_type=F32)

    y = y_ref[...] * z_ref[...].astype(F32)
    outs = []
    for g in range(SSD_GROUPS):
        yg = y[:, g * GROUP_WIDTH:(g + 1) * GROUP_WIDTH]
        outs.append(yg * lax.rsqrt(jnp.mean(yg * yg, axis=-1, keepdims=True) + EPS))
    yn = jnp.concatenate(outs, axis=-1) * nrm_ref[...]
    slot = lax.rem(c, CPT)
    yn_ref[pl.ds(pl.multiple_of(slot * CHUNK, CHUNK), CHUNK), :] = yn.astype(BF16)

    @pl.when(slot == CPT - 1)
    def _():
        yo = jnp.dot(yn_ref[...], wo_ref[...], preferred_element_type=F32)
        o_ref[...] = (gs_ref[...].astype(F32) * yo).astype(BF16)


def _ssd_branch(z_act, x_act, bc_act, dt_raw, gates, dt_bias, a_log, d_skip, ssd_norm, w_ssd_out, bsz, lp):
    rows = z_act.shape[0]
    nc = lp // CHUNK
    tiles = nc // CPT
    chunk = lambda b, c: (b * nc + c, 0)
    const = lambda b, c: (0, 0)
    return pl.pallas_call(
        _ssd_kernel,
        grid=(bsz, nc),
        in_specs=[pl.BlockSpec((CHUNK, SSD_INNER), chunk),
                  pl.BlockSpec((CHUNK, SSD_INNER), chunk),
                  pl.BlockSpec((CHUNK, BC_WIDTH), chunk),
                  pl.BlockSpec((CHUNK, LANES), chunk),
                  pl.BlockSpec((TM, D_MODEL), lambda b, c: (b * tiles + c // CPT, 1)),
                  pl.BlockSpec((1, LANES), const),
                  pl.BlockSpec((1, LANES), const),
                  pl.BlockSpec((1, SSD_INNER), const),
                  pl.BlockSpec((1, SSD_INNER), const),
                  pl.BlockSpec((SSD_INNER, D_MODEL), const)],
        out_specs=pl.BlockSpec((TM, D_MODEL), lambda b, c: (b * tiles + c // CPT, 0)),
        out_shape=jax.ShapeDtypeStruct((rows, D_MODEL), BF16),
        scratch_shapes=[pltpu.VMEM((CHUNK, SSD_INNER), F32),
                        pltpu.VMEM((SSD_GROUPS, SSD_STATE, GROUP_WIDTH), F32),
                        pltpu.VMEM((TM, SSD_INNER), BF16)],
        compiler_params=_cparams("parallel", "arbitrary"),
        name="ssd_branch",
    )(z_act, x_act, bc_act, dt_raw, gates, dt_bias, a_log, d_skip, ssd_norm, w_ssd_out)


ROUTE_ROWS = 8
NT_DIMS = (((1,), (1,)), ((), ()))


def _first_row_of_max(vals, rowid, nrows):
    m = jnp.max(vals, axis=0, keepdims=True)
    idx = jnp.min(jnp.where(vals == m, rowid, nrows), axis=0, keepdims=True)
    return m, idx


def _merge_kernel(h_ref, yp_ref, ys_ref, wo_ref, nf_ref, wrh_ref, wrl_ref, br_ref,
                  h1_ref, hn_ref, route_ref, cnt_ref, carry, *, lp):
    i = pl.program_id(0)

    @pl.when(i == 0)
    def _():
        carry[...] = jnp.zeros_like(carry)

    merged = yp_ref[...].astype(F32) + ys_ref[...].astype(F32)
    h1 = h_ref[...] + jnp.dot(merged.astype(BF16), wo_ref[...], preferred_element_type=F32)
    h1_ref[...] = h1
    hn = h1 * lax.rsqrt(jnp.mean(h1 * h1, axis=-1, keepdims=True) + EPS) * nf_ref[...]
    hn_ref[...] = hn

    hn_hi = hn.astype(BF16)
    hn_lo = (hn - hn_hi.astype(F32)).astype(BF16)
    logits = (lax.dot_general(wrh_ref[...], hn_hi, NT_DIMS, preferred_element_type=F32)
              + lax.dot_general(wrl_ref[...], hn_hi, NT_DIMS, preferred_element_type=F32)
              + lax.dot_general(wrh_ref[...], hn_lo, NT_DIMS, preferred_element_type=F32)) + br_ref[...]
    gl = logits[0:N_EXPERT_GROUPS, :]
    grow = lax.broadcasted_iota(I32, (N_EXPERT_GROUPS, 1), 0)
    gmax, gidx = _first_row_of_max(gl, grow, N_EXPERT_GROUPS)
    g_p = 1.0 / jnp.sum(jnp.exp(gl - gmax), axis=0, keepdims=True)
    erow = lax.broadcasted_iota(I32, (N_EXPERTS, 1), 0)
    el_all = logits[N_EXPERT_GROUPS:N_EXPERT_GROUPS + N_EXPERTS, :]
    lo_row = gidx * EXPERTS_PER_GROUP
    el = jnp.where((erow >= lo_row) & (erow < lo_row + EXPERTS_PER_GROUP), el_all, -jnp.inf)
    m1, e1 = _first_row_of_max(el, erow, N_EXPERTS)
    m2, e2 = _first_row_of_max(jnp.where(erow == e1, -jnp.inf, el), erow, N_EXPERTS)
    t = jnp.exp(m2 - m1)
    w1 = g_p / (1.0 + t)
    w2 = g_p * t / (1.0 + t)

    coli = lax.broadcasted_iota(I32, (1, TM), 1)
    valid = (lax.rem(i * TM, lp) + coli) >= LEAD
    xrow = lax.broadcasted_iota(I32, (LANES, 1), 0)
    o1 = jnp.where((xrow == e1) & valid, 1.0, 0.0)
    o2 = jnp.where((xrow == e2) & valid, 1.0, 0.0)
    both = o1 + o2
    earlier = (lax.broadcasted_iota(I32, (TM, 1), 0) < coli).astype(BF16)
    before = jnp.dot(both.astype(BF16), earlier, preferred_element_type=F32) + carry[...]
    r1 = jnp.sum(o1 * before, axis=0, keepdims=True)
    r2 = jnp.sum(o2 * before, axis=0, keepdims=True)
    carry[...] = carry[...] + jnp.sum(both, axis=1, keepdims=True)
    cnt_ref[...] = jnp.broadcast_to(carry[...], cnt_ref.shape)

    zero = jnp.zeros_like(w1)
    route_ref[...] = jnp.concatenate(
        [e1.astype(F32), e2.astype(F32), r1, r2, w1, w2, zero, zero], axis=0)


def _merge_router(hp, yp, ys, w_out, norm_ffn, wr_hi, wr_lo, b_router, lp):
    rows = hp.shape[0]
    tile = lambda i: (i, 0)
    const = lambda i: (0, 0)
    return pl.pallas_call(
        functools.partial(_merge_kernel, lp=lp),
        grid=(rows // TM,),
        in_specs=[pl.BlockSpec((TM, D_MODEL), tile),
                  pl.BlockSpec((TM, D_MODEL), tile),
                  pl.BlockSpec((TM, D_MODEL), tile),
                  pl.BlockSpec((D_MODEL, D_MODEL), const),
                  pl.BlockSpec((1, D_MODEL), const),
                  pl.BlockSpec((LANES, D_MODEL), const),
                  pl.BlockSpec((LANES, D_MODEL), const),
                  pl.BlockSpec((LANES, TM), const)],
        out_specs=[pl.BlockSpec((TM, D_MODEL), tile),
                   pl.BlockSpec((TM, D_MODEL), tile),
                   pl.BlockSpec((ROUTE_ROWS, TM), lambda i: (0, i)),
                   pl.BlockSpec((LANES, LANES), const)],
        out_shape=[jax.ShapeDtypeStruct((rows, D_MODEL), F32),
                   jax.ShapeDtypeStruct((rows, D_MODEL), F32),
                   jax.ShapeDtypeStruct((ROUTE_ROWS, rows), F32),
                   jax.ShapeDtypeStruct((LANES, LANES), F32)],
        scratch_shapes=[pltpu.VMEM((LANES, 1), F32)],
        compiler_params=_cparams("arbitrary"),
        name="merge_router",
    )(hp, yp, ys, w_out, norm_ffn, wr_hi, wr_lo, b_router)


def _row_copy(src, src_row, dst, dst_row, sem):
    return pltpu.make_async_copy(src.at[pl.ds(src_row, 1)], dst.at[pl.ds(dst_row, 1)], sem)


def _wait_rows(buf, nrows, sem):
    view = buf.at[pl.ds(0, nrows)]
    pltpu.make_async_copy(view, view, sem).wait()


def _dispatch_kernel(zb_ref, dest_ref, hn_ref, xs_out, zbuf, sem, zsem):
    first = pl.program_id(1) == 0

    @pl.when((pl.program_id(0) == 0) & first)
    def _():
        zbuf[...] = jnp.zeros_like(zbuf)

        def block_copy(j):
            row0 = pl.multiple_of(jnp.maximum(zb_ref[j], 0) * MOE_BLOCK, MOE_BLOCK)
            return pltpu.make_async_copy(zbuf, xs_out.at[pl.ds(row0, MOE_BLOCK)], zsem)

        def start(j, carry):
            @pl.when(zb_ref[j] >= 0)
            def _():
                block_copy(j).start()
            return carry

        def finish(j, carry):
            @pl.when(zb_ref[j] >= 0)
            def _():
                block_copy(j).wait()
            return carry

        lax.fori_loop(0, ZERO_BLOCKS, start, 0)
        lax.fori_loop(0, ZERO_BLOCKS, finish, 0)

    def issue(blk, lo):
        def body(r, carry):
            for k in range(2):
                _row_copy(hn_ref, blk * CHUNK + r, xs_out, dest_ref[blk, k, r], sem).start()
            return carry
        lax.fori_loop(lo, CHUNK, body, 0, unroll=DMA_UNROLL)

    @pl.when(first)
    def _():
        issue(0, LEAD)

    @pl.when(jnp.logical_not(first))
    def _():
        issue(0, 0)

    for blk in range(1, CPT):
        issue(blk, 0)

    @pl.when(first)
    def _():
        for _k in range(2):
            _wait_rows(hn_ref, TM - LEAD, sem)

    @pl.when(jnp.logical_not(first))
    def _():
        for _k in range(2):
            _wait_rows(hn_ref, TM, sem)


def _dispatch(zero_blocks, dest, hn, nrows, bsz, lp):
    tiles = lp // TM
    return pl.pallas_call(
        _dispatch_kernel,
        grid_spec=pltpu.PrefetchScalarGridSpec(
            num_scalar_prefetch=1,
            grid=(bsz, tiles),
            in_specs=[pl.BlockSpec((CPT, 2, CHUNK), lambda b, c, zb: (b * tiles + c, 0, 0),
                                   memory_space=pltpu.SMEM),
                      pl.BlockSpec((TM, D_MODEL), lambda b, c, zb: (b * tiles + c, 0))],
            out_specs=pl.BlockSpec(memory_space=pl.ANY),
            scratch_shapes=[pltpu.VMEM((MOE_BLOCK, D_MODEL), F32),
                            pltpu.SemaphoreType.DMA(()),
                            pltpu.SemaphoreType.DMA(())]),
        out_shape=jax.ShapeDtypeStruct((nrows, D_MODEL), F32),
        compiler_params=_cparams("arbitrary", "arbitrary"),
        name="moe_dispatch",
    )(zero_blocks, dest, hn)


def _expert_kernel(blk0_ref, nblk_ref, nused_ref, x_hbm, wgu_ref, wdn_ref, y_hbm,
                   xbuf, ybuf, wgu16, wdn16, xsem, ysem):
    e = pl.program_id(0)
    nblk = nblk_ref[e]
    nused = nused_ref[0]
    nb = y_hbm.shape[0] // MOE_BLOCK

    def x_copy(g, slot):
        rows = pl.ds(pl.multiple_of(g * MOE_BLOCK, MOE_BLOCK), MOE_BLOCK)
        return pltpu.make_async_copy(x_hbm.at[rows], xbuf.at[slot], xsem.at[slot])

    def y_copy(g, slot):
        rows = pl.ds(pl.multiple_of(g * MOE_BLOCK, MOE_BLOCK), MOE_BLOCK)
        return pltpu.make_async_copy(ybuf.at[slot], y_hbm.at[rows], ysem.at[slot])

    @pl.when((e == 0) & (nused > 0))
    def _():
        x_copy(0, 0).start()

    @pl.when(nblk > 0)
    def _():
        wgu16[...] = wgu_ref[0].astype(BF16)
        wdn16[...] = wdn_ref[0].astype(BF16)

        def body(j, carry):
            g = blk0_ref[e] + j
            slot = lax.rem(g, 2)

            @pl.when(g + 1 < nused)
            def _():
                x_copy(g + 1, 1 - slot).start()

            x_copy(g, slot).wait()

            @pl.when(g >= 2)
            def _():
                y_copy(g - 2, slot).wait()

            gu = jnp.dot(xbuf[slot].astype(BF16), wgu16[...], preferred_element_type=F32)
            gate, up = gu[:, :D_EXPERT], gu[:, D_EXPERT:]
            act = gate * _sigmoid(gate) * up
            ybuf[slot] = jnp.dot(act.astype(BF16), wdn16[...], preferred_element_type=F32)
            y_copy(g, slot).start()
            return carry

        lax.fori_loop(0, nblk, body, 0)

    @pl.when(e == pl.num_programs(0) - 1)
    def _():
        @pl.when(nused >= 2)
        def _():
            y_copy(nused - 2, lax.rem(nused, 2)).wait()

        @pl.when(nused >= 1)
        def _():
            y_copy(nused - 1, lax.rem(nused + 1, 2)).wait()

        ybuf[0] = jnp.zeros((MOE_BLOCK, D_MODEL), F32)

        def zero_start(g, carry):
            y_copy(g, 0).start()
            return carry

        def zero_wait(g, carry):
            y_copy(g, 0).wait()
            return carry

        lax.fori_loop(nused, nb, zero_start, 0)
        lax.fori_loop(nused, nb, zero_wait, 0)


def _experts(blk0, nblk, nused, xs, w_gu, w_dn):
    wsel = lambda e, b0, nbk, nu: (e, 0, 0)
    return pl.pallas_call(
        _expert_kernel,
        grid_spec=pltpu.PrefetchScalarGridSpec(
            num_scalar_prefetch=3,
            grid=(N_EXPERTS,),
            in_specs=[pl.BlockSpec(memory_space=pl.ANY),
                      pl.BlockSpec((1, D_MODEL, 2 * D_EXPERT), wsel),
                      pl.BlockSpec((1, D_EXPERT, D_MODEL), wsel)],
            out_specs=pl.BlockSpec(memory_space=pl.ANY),
            scratch_shapes=[pltpu.VMEM((2, MOE_BLOCK, D_MODEL), F32),
                            pltpu.VMEM((2, MOE_BLOCK, D_MODEL), F32),
                            pltpu.VMEM((D_MODEL, 2 * D_EXPERT), BF16),
                            pltpu.VMEM((D_EXPERT, D_MODEL), BF16),
                            pltpu.SemaphoreType.DMA((2,)),
                            pltpu.SemaphoreType.DMA((2,))]),
        out_shape=jax.ShapeDtypeStruct(xs.shape, F32),
        compiler_params=_cparams("arbitrary"),
        name="moe_experts",
    )(blk0, nblk, nused, xs, w_gu, w_dn)


def _combine_kernel(da_ref, db_ref, dn_ref, h1a_ref, h1b_ref, rta_ref, rtb_ref, nf_ref, y_hbm,
                    o_ref, ya, yb, sem):
    step = pl.program_id(0) * pl.num_programs(1) + pl.program_id(1)
    nsteps = pl.num_programs(0) * pl.num_programs(1)

    def issue(d_ref, slot):
        def body(r, carry):
            _row_copy(y_hbm, d_ref[0, 0, r], ya.at[slot], r, sem.at[slot, 0]).start()
            _row_copy(y_hbm, d_ref[0, 1, r], yb.at[slot], r, sem.at[slot, 1]).start()
            return carry
        lax.fori_loop(0, CHUNK, body, 0, unroll=DMA_UNROLL)

    def finish(slot, h1_ref, rt_ref):
        _wait_rows(ya.at[slot], CHUNK, sem.at[slot, 0])
        _wait_rows(yb.at[slot], CHUNK, sem.at[slot, 1])
        rt = rt_ref[...]
        h2 = h1_ref[...] + rt[:, 0:1] * ya[slot] + rt[:, 1:2] * yb[slot]
        o_ref[0, slot * CHUNK:(slot + 1) * CHUNK, :] = (
            h2 * lax.rsqrt(jnp.mean(h2 * h2, axis=-1, keepdims=True) + EPS) * nf_ref[...])

    @pl.when(step == 0)
    def _():
        issue(da_ref, 0)

    issue(db_ref, 1)
    finish(0, h1a_ref, rta_ref)

    @pl.when(step + 1 < nsteps)
    def _():
        issue(dn_ref, 0)

    finish(1, h1b_ref, rtb_ref)


def _combine(dest, h1, wts, norm_final, ys, bsz, lp, seq):
    nc = lp // CHUNK
    half = seq // (2 * CHUNK)
    first = lambda b, j: b * nc + 1 + 2 * j
    second = lambda b, j: b * nc + 2 + 2 * j

    def upcoming(b, j):
        nxt = jnp.minimum(b * half + j + 1, bsz * half - 1)
        return first(nxt // half, nxt % half)

    smem3 = lambda f: pl.BlockSpec((1, 2, CHUNK), lambda b, j: (f(b, j), 0, 0), memory_space=pltpu.SMEM)
    rows2 = lambda f, width: pl.BlockSpec((CHUNK, width), lambda b, j: (f(b, j), 0))
    return pl.pallas_call(
        _combine_kernel,
        grid=(bsz, half),
        in_specs=[smem3(first), smem3(second), smem3(upcoming),
                  rows2(first, D_MODEL), rows2(second, D_MODEL),
                  rows2(first, 2), rows2(second, 2),
                  pl.BlockSpec((1, D_MODEL), lambda b, j: (0, 0)),
                  pl.BlockSpec(memory_space=pl.ANY)],
        out_specs=pl.BlockSpec((1, 2 * CHUNK, D_MODEL), lambda b, j: (b, j, 0)),
        out_shape=jax.ShapeDtypeStruct((bsz, seq, D_MODEL), F32),
        scratch_shapes=[pltpu.VMEM((2, CHUNK, D_MODEL), F32),
                        pltpu.VMEM((2, CHUNK, D_MODEL), F32),
                        pltpu.SemaphoreType.DMA((2, 2))],
        compiler_params=_cparams("arbitrary", "arbitrary"),
        name="moe_combine",
    )(dest, dest, dest, h1, h1, wts, wts, norm_final, ys)


def _row(v):
    return v.reshape(1, -1).astype(F32)


def _pad_lanes(v):
    return jnp.pad(v, ((0, 0), (0, LANES - v.shape[1])))


def kernel(x, meta_tokens, norm_mix, w_in, pool_w, pool_scale, conv_w, conv_b, dt_bias, a_log, d_skip,
           ssd_norm, w_pool_out, w_ssd_out, w_out, norm_ffn, w_router_group, b_router_group,
           w_router_expert, b_router_expert, w_gate_up, w_down, norm_final):
    bsz, seq, _ = x.shape
    lp = LEAD + N_META + seq
    assert lp % TM == 0 and seq % (2 * CHUNK) == 0
    rows = bsz * lp

    meta = jnp.broadcast_to(meta_tokens.astype(x.dtype)[None], (bsz, N_META, D_MODEL))
    hp = jnp.concatenate([jnp.zeros((bsz, LEAD, D_MODEL), x.dtype), meta, x], axis=1)
    hp = hp.reshape(rows, D_MODEL)

    wi = w_in[0]
    o_z, o_x, o_bc, o_dt, o_gp = 1024, 3072, 5120, 6144, 6176
    gain = _row(norm_mix[0])
    half_w = lambda cols: (0.5 * cols).astype(BF16)
    cw_half = 0.5 * conv_w[0].astype(F32)
    cb_half = 0.5 * _row(conv_b[0])
    (z_act,) = _proj_call(_proj_silu_kernel, hp, gain, (half_w(wi[:, o_z:o_x]),),
                          [(SSD_INNER, BF16)], name="proj_z")
    (x_act,) = _proj_conv_call(hp, gain, wi[:, o_x:o_bc].astype(BF16), cw_half[:, :SSD_INNER],
                               cb_half[:, :SSD_INNER], lp, SSD_INNER, 0, "proj_x")
    w_bcu = jnp.concatenate([wi[:, o_bc:o_dt], wi[:, :o_z]], axis=1).astype(BF16)
    bc_act, u = _proj_conv_call(hp, gain, w_bcu, cw_half[:, SSD_INNER:], cb_half[:, SSD_INNER:],
                                lp, BC_WIDTH, D_MODEL, "proj_bcu")
    gates, dt_raw = _proj_call(_proj_gate_kernel, hp, gain,
                               (half_w(wi[:, o_gp:]), _pad_lanes(wi[:, o_dt:o_gp]).astype(BF16)),
                               [(2 * D_MODEL, BF16), (LANES, F32)], name="proj_gates")

    yp = _pool_branch(u, gates, pool_w[0].astype(BF16), _row(pool_scale[0]),
                      w_pool_out[0].astype(BF16), lp)
    ys = _ssd_branch(z_act, x_act, bc_act, dt_raw, gates,
                     _pad_lanes(_row(dt_bias[0])), _pad_lanes(_row(a_log[0])),
                     _row(jnp.repeat(d_skip[0], SSD_HEAD_DIM)), _row(ssd_norm[0]),
                     w_ssd_out[0].astype(BF16), bsz, lp)

    w_router = jnp.concatenate([w_router_group[0], w_router_expert[0]], axis=1).astype(F32).T
    w_router = jnp.pad(w_router, ((0, LANES - w_router.shape[0]), (0, 0)))
    wr_hi = w_router.astype(BF16)
    wr_lo = (w_router - wr_hi.astype(F32)).astype(BF16)
    b_router = _pad_lanes(jnp.concatenate([_row(b_router_group[0]), _row(b_router_expert[0])], axis=1))
    b_router = jnp.broadcast_to(b_router.reshape(LANES, 1), (LANES, TM))
    h1, hn, route, counts = _merge_router(hp, yp, ys, w_out[0].astype(BF16), _row(norm_ffn[0]),
                                          wr_hi, wr_lo, b_router, lp)

    n_assign = 2 * bsz * (N_META + seq)
    nb = -(-n_assign // MOE_BLOCK) + N_EXPERTS
    cnt = counts[:N_EXPERTS, 0].astype(I32)
    pcnt = (cnt + MOE_BLOCK - 1) // MOE_BLOCK * MOE_BLOCK
    pends = jnp.cumsum(pcnt)
    pstarts = pends - pcnt
    experts = route[0:2].astype(I32)
    eids = jnp.arange(N_EXPERTS, dtype=I32)[:, None]
    start_of = jnp.sum(jnp.where(experts[:, None, :] == eids, pstarts[:, None], 0), axis=1)
    dest = start_of + route[2:4].astype(I32)
    dest = dest.reshape(2, rows // CHUNK, CHUNK).transpose(1, 0, 2)
    wts = route[4:6].T
    nused = (pends[-1:] // MOE_BLOCK).astype(I32)
    last_blocks = jnp.where(pcnt > 0, pends // MOE_BLOCK - 1, -1)
    tail_ids = nused[0] + jnp.arange(ZERO_BLOCKS - N_EXPERTS, dtype=I32)
    zero_blocks = jnp.concatenate([last_blocks, jnp.where(tail_ids < nb, tail_ids, -1)]).astype(I32)

    xs = _dispatch(zero_blocks, dest, hn, nb * MOE_BLOCK, bsz, lp)
    yexp = _experts((pstarts // MOE_BLOCK).astype(I32), (pcnt // MOE_BLOCK).astype(I32), nused,
                    xs, w_gate_up[0], w_down[0])
    return _combine(dest, h1, wts, _row(norm_final), yexp, bsz, lp, seq)
```

```python
import functools
import math

import jax
import jax.numpy as jnp
from jax import lax
from jax.experimental import pallas as pl
from jax.experimental.pallas import tpu as pltpu

F32 = jnp.float32
BF16 = jnp.bfloat16
I32 = jnp.int32

D_MODEL = 1024
N_META = 16
EPS = 1e-6
LOG2E = math.log2(math.e)
CHUNK = 128
LEAD = CHUNK - N_META
CPT = 5
TM = CPT * CHUNK
POOL_WINDOWS = (2, 4, 8, 16)
POOL_GROUP = D_MODEL // len(POOL_WINDOWS)
POOL_HALO = 16
SSD_INNER = 2 * D_MODEL
SSD_HEADS = 32
SSD_HEAD_DIM = 64
SSD_GROUPS = 4
SSD_STATE = 128
SSD_CONV = 4
CONV_HALO = 8
BC_WIDTH = 2 * SSD_GROUPS * SSD_STATE
GROUP_WIDTH = SSD_INNER // SSD_GROUPS
PAIRS_PER_GROUP = GROUP_WIDTH // 128
N_EXPERT_GROUPS = 8
EXPERTS_PER_GROUP = 8
N_EXPERTS = 64
D_EXPERT = 512
MOE_BLOCK = 512
LANES = 128
DMA_UNROLL = 8
ROW_TILE = D_MODEL // LANES
ZERO_BLOCKS = 2 * N_EXPERTS
STRIP = 256
GROUP_X = SSD_INNER // SSD_GROUPS
GROUP_CONV = GROUP_X + 2 * SSD_STATE
GROUP_Z = SSD_INNER // SSD_GROUPS
GROUP_GATE = 2 * D_MODEL // SSD_GROUPS
GROUP_U = D_MODEL // SSD_GROUPS
GROUP_COLS = GROUP_CONV + GROUP_Z + GROUP_GATE + GROUP_U

VMEM_LIMIT = 48 * 1024 * 1024


def _cparams(*sem):
    return pltpu.CompilerParams(dimension_semantics=sem, vmem_limit_bytes=VMEM_LIMIT)


def _sigmoid(x):
    return 1.0 / (1.0 + jnp.exp(-x))


def _silu_of_half(h):
    return h + h * jnp.tanh(h)


def _normed(h_ref, g_ref):
    x = h_ref[...]
    return (x * lax.rsqrt(jnp.mean(x * x, axis=-1, keepdims=True) + EPS) * g_ref[...]).astype(BF16)


def _proj_kernel(h_ref, g_ref, w_ref, cw_ref, cb_ref, wdt_ref,
                 xbc_ref, z_ref, gate_ref, u_ref, dt_ref, pbuf, *, lp):
    grp, i = pl.program_id(0), pl.program_id(1)

    @pl.when(i == 0)
    def _():
        pbuf[0:CONV_HALO, :] = jnp.zeros((CONV_HALO, GROUP_CONV), F32)

    hn = _normed(h_ref, g_ref)
    proj = lambda c0, n: jnp.dot(hn, w_ref[0, :, c0:c0 + n], preferred_element_type=F32)

    row = lax.rem(i * TM, lp) + lax.broadcasted_iota(I32, (CHUNK, 1), 0)
    lead_mask = row >= LEAD
    for c0 in range(0, GROUP_CONV, STRIP):
        cols = slice(c0, c0 + STRIP)
        pbuf[CONV_HALO:, cols] = proj(c0, STRIP)
        for r0 in range(0, TM, CHUNK):
            xa = pbuf[r0:r0 + CONV_HALO + CHUNK, cols]
            half = cb_ref[0, :, cols] + cw_ref[0, SSD_CONV - 1:SSD_CONV, cols] * xa[CONV_HALO:, :]
            for s in range(1, SSD_CONV):
                tap = cw_ref[0, SSD_CONV - 1 - s:SSD_CONV - s, cols]
                half = half + tap * pltpu.roll(xa, s, axis=0)[CONV_HALO:, :]
            act = _silu_of_half(half)
            if r0 == 0:
                act = jnp.where(lead_mask, act, 0.0)
            xbc_ref[0, r0:r0 + CHUNK, cols] = act.astype(BF16)
    pbuf[0:CONV_HALO, :] = pbuf[TM:TM + CONV_HALO, :]

    for c0 in range(0, GROUP_Z, STRIP):
        z_ref[0, :, c0:c0 + STRIP] = _silu_of_half(proj(GROUP_CONV + c0, STRIP)).astype(BF16)
    for c0 in range(0, GROUP_GATE, STRIP):
        half = proj(GROUP_CONV + GROUP_Z + c0, STRIP)
        gate_ref[0, :, c0:c0 + STRIP] = (0.5 + 0.5 * jnp.tanh(half)).astype(BF16)
    u_ref[0] = proj(GROUP_CONV + GROUP_Z + GROUP_GATE, GROUP_U)

    @pl.when(grp == 0)
    def _():
        dt_ref[...] = jnp.dot(hn, wdt_ref[...], preferred_element_type=F32)


def _projections(hp, gain, w_groups, cw_groups, cb_groups, w_dt, lp):
    rows = hp.shape[0]
    tiles = rows // TM
    per_group = lambda n: pl.BlockSpec((1, TM, n), lambda g, i: (g, i, 0))
    weights = lambda a: pl.BlockSpec((1,) + a.shape[1:], lambda g, i: (g, 0, 0))
    dt_spec = pl.BlockSpec((TM, LANES), lambda g, i: (jnp.where(g == 0, i, tiles - 1), 0))
    return pl.pallas_call(
        functools.partial(_proj_kernel, lp=lp),
        grid=(SSD_GROUPS, tiles),
        in_specs=[pl.BlockSpec((TM, D_MODEL), lambda g, i: (i, 0)),
                  pl.BlockSpec((1, D_MODEL), lambda g, i: (0, 0)),
                  weights(w_groups), weights(cw_groups), weights(cb_groups),
                  pl.BlockSpec(w_dt.shape, lambda g, i: (0, 0))],
        out_specs=[per_group(GROUP_CONV), per_group(GROUP_Z), per_group(GROUP_GATE), per_group(GROUP_U),
                   dt_spec],
        out_shape=[jax.ShapeDtypeStruct((SSD_GROUPS, rows, GROUP_CONV), BF16),
                   jax.ShapeDtypeStruct((SSD_GROUPS, rows, GROUP_Z), BF16),
                   jax.ShapeDtypeStruct((SSD_GROUPS, rows, GROUP_GATE), BF16),
                   jax.ShapeDtypeStruct((SSD_GROUPS, rows, GROUP_U), F32),
                   jax.ShapeDtypeStruct((rows, LANES), F32)],
        scratch_shapes=[pltpu.VMEM((CONV_HALO + TM, GROUP_CONV), F32)],
        compiler_params=_cparams("arbitrary", "arbitrary"),
        name="projections",
    )(hp, gain, w_groups, cw_groups, cb_groups, w_dt)


def _pool_kernel(u_ref, uprev_ref, gate_ref, pw_ref, ps_ref, wo_ref, o_ref, ubuf, *, lp):
    i = pl.program_id(0)
    ubuf[:, 0:POOL_HALO, :] = uprev_ref[...]
    ubuf[:, POOL_HALO:POOL_HALO + TM, :] = u_ref[...]
    row = lax.rem(i * TM, lp) + lax.broadcasted_iota(I32, (TM, 1), 0)
    tpos = row - LEAD
    parts = []
    for k, w in enumerate(POOL_WINDOWS):
        cur = ubuf[k, POOL_HALO:POOL_HALO + TM, :]
        acc = cur
        for s in range(1, w):
            acc = acc + ubuf[k, POOL_HALO - s:POOL_HALO - s + TM, :]
        cnt = jnp.clip(tpos + 1, 1, w).astype(F32)
        d = acc / cnt - cur
        parts.append(jnp.dot(d.astype(BF16), pw_ref[k], preferred_element_type=F32))
    pm = jnp.concatenate(parts, axis=-1) * ps_ref[...]
    y = jnp.dot(pm.astype(BF16), wo_ref[...], preferred_element_type=F32)
    gate = jnp.concatenate([gate_ref[k, :, 0:POOL_GROUP] for k in range(len(POOL_WINDOWS))], axis=-1)
    o_ref[...] = (gate.astype(F32) * y).astype(BF16)


def _pool_branch(u, gates, pool_w, pool_scale, w_pool_out, lp):
    ngroups, rows = u.shape[0], u.shape[1]
    halo_blocks = TM // POOL_HALO
    return pl.pallas_call(
        functools.partial(_pool_kernel, lp=lp),
        grid=(rows // TM,),
        in_specs=[pl.BlockSpec((ngroups, TM, POOL_GROUP), lambda i: (0, i, 0)),
                  pl.BlockSpec((ngroups, POOL_HALO, POOL_GROUP),
                               lambda i: (0, jnp.maximum(i * halo_blocks - 1, 0), 0)),
                  pl.BlockSpec((ngroups, TM, GROUP_GATE), lambda i: (0, i, 0)),
                  pl.BlockSpec((len(POOL_WINDOWS), POOL_GROUP, POOL_GROUP), lambda i: (0, 0, 0)),
                  pl.BlockSpec((1, D_MODEL), lambda i: (0, 0)),
                  pl.BlockSpec((D_MODEL, D_MODEL), lambda i: (0, 0))],
        out_specs=pl.BlockSpec((TM, D_MODEL), lambda i: (i, 0)),
        out_shape=jax.ShapeDtypeStruct((rows, D_MODEL), BF16),
        scratch_shapes=[pltpu.VMEM((ngroups, POOL_HALO + TM, POOL_GROUP), F32)],
        compiler_params=_cparams("parallel"),
        name="pool_branch",
    )(u, u, gates, pool_w, pool_scale, w_pool_out)


def _pair_lanes(cols, h0, left):
    return jnp.where(left, cols[:, h0:h0 + 1], cols[:, h0 + 1:h0 + 2])


def _ssd_kernel(z_ref, xbc_ref, dt_ref, gate_ref, dtb_ref, alog_ref, dsk_ref, nrm_ref, wo_ref,
                o_ref, y_ref, st_ref, yn_ref):
    c = pl.program_id(1)

    @pl.when(c == 0)
    def _():
        st_ref[...] = jnp.zeros_like(st_ref)

    rowi = lax.broadcasted_iota(I32, (CHUNK, 1), 0)
    coli = lax.broadcasted_iota(I32, (1, CHUNK), 1)
    causal = rowi >= coli
    left = coli < SSD_HEAD_DIM

    dtr = dt_ref[...] + dtb_ref[...]
    dt = jnp.maximum(dtr, 0.0) + jnp.log(1.0 + jnp.exp(-jnp.abs(dtr)))
    dt = jnp.where(rowi >= jnp.where(c == 0, LEAD, 0), dt, 0.0)
    adt2 = dt * (-LOG2E * jnp.exp(alog_ref[...]))
    a2 = jnp.dot(causal.astype(F32), adt2, precision=lax.Precision.HIGHEST,
                 preferred_element_type=F32)
    a2_t = a2.T

    for g in range(SSD_GROUPS):
        b_g = xbc_ref[g, :, GROUP_X:GROUP_X + SSD_STATE]
        c_g = xbc_ref[g, :, GROUP_X + SSD_STATE:GROUP_CONV]
        cb = lax.dot_general(c_g, b_g, (((1,), (1,)), ((), ())), preferred_element_type=F32)
        b_gt = b_g.T
        st_g = st_ref[g]
        y_off = jnp.dot(c_g, st_g.astype(BF16), preferred_element_type=F32)
        xw_parts, dec_parts = [], []
        for q in range(PAIRS_PER_GROUP):
            pair = g * PAIRS_PER_GROUP + q
            h0 = 2 * pair
            lanes = slice(pair * LANES, (pair + 1) * LANES)
            xq = xbc_ref[g, :, q * LANES:(q + 1) * LANES].astype(F32)
            a_cols = [jnp.broadcast_to(a2[:, h:h + 1], (CHUNK, CHUNK)) for h in (h0, h0 + 1)]
            a_pair = jnp.where(left, a_cols[0], a_cols[1])
            dt_pair = _pair_lanes(dt, h0, left)
            a_pair_last = a_pair[CHUNK - 1:CHUNK, :]
            xdt = (xq * dt_pair).astype(BF16)
            heads = []
            for j, h in enumerate((h0, h0 + 1)):
                seg = jnp.where(causal, a_cols[j] - a2_t[h:h + 1, :], -jnp.inf)
                m = (cb * jnp.exp2(seg)).astype(BF16)
                heads.append(jnp.dot(m, xdt, preferred_element_type=F32))
            y_ref[:, lanes] = (jnp.where(left, heads[0], heads[1])
                               + y_off[:, q * LANES:(q + 1) * LANES] * jnp.exp2(a_pair)
                               + dsk_ref[:, lanes] * xq)
            xw_parts.append((xq * (dt_pair * jnp.exp2(a_pair_last - a_pair))).astype(BF16))
            dec_parts.append(jnp.exp2(a_pair_last))
        xw = jnp.concatenate(xw_parts, axis=-1)
        dec = jnp.concatenate(dec_parts, axis=-1)
        st_ref[g] = st_g * dec + jnp.dot(b_gt, xw, preferred_element_type=F32)

    outs = []
    for g in range(SSD_GROUPS):
        yg = y_ref[:, g * GROUP_WIDTH:(g + 1) * GROUP_WIDTH] * z_ref[g].astype(F32)
        outs.append(yg * lax.rsqrt(jnp.mean(yg * yg, axis=-1, keepdims=True) + EPS))
    yn = jnp.concatenate(outs, axis=-1) * nrm_ref[...]
    slot = lax.rem(c, CPT)
    yn_ref[pl.ds(pl.multiple_of(slot * CHUNK, CHUNK), CHUNK), :] = yn.astype(BF16)

    @pl.when(slot == CPT - 1)
    def _():
        yo = jnp.dot(yn_ref[...], wo_ref[...], preferred_element_type=F32)
        gate = jnp.concatenate([gate_ref[k, :, GROUP_GATE // 2:] for k in range(SSD_GROUPS)], axis=-1)
        o_ref[...] = (gate.astype(F32) * yo).astype(BF16)


def _ssd_branch(z_act, xbc_act, dt_raw, gates, dt_bias, a_log, d_skip, ssd_norm, w_ssd_out, bsz, lp):
    rows = dt_raw.shape[0]
    nc = lp // CHUNK
    tiles = nc // CPT
    grouped = lambda nrows, n, f: pl.BlockSpec((SSD_GROUPS, nrows, n), lambda b, c: (0, f(b, c), 0))
    chunk = lambda b, c: b * nc + c
    tile = lambda b, c: b * tiles + c // CPT
    const = lambda b, c: (0, 0)
    return pl.pallas_call(
        _ssd_kernel,
        grid=(bsz, nc),
        in_specs=[grouped(CHUNK, GROUP_Z, chunk),
                  grouped(CHUNK, GROUP_CONV, chunk),
                  pl.BlockSpec((CHUNK, LANES), lambda b, c: (chunk(b, c), 0)),
                  grouped(TM, GROUP_GATE, tile),
                  pl.BlockSpec((1, LANES), const),
                  pl.BlockSpec((1, LANES), const),
                  pl.BlockSpec((1, SSD_INNER), const),
                  pl.BlockSpec((1, SSD_INNER), const),
                  pl.BlockSpec((SSD_INNER, D_MODEL), const)],
        out_specs=pl.BlockSpec((TM, D_MODEL), lambda b, c: (tile(b, c), 0)),
        out_shape=jax.ShapeDtypeStruct((rows, D_MODEL), BF16),
        scratch_shapes=[pltpu.VMEM((CHUNK, SSD_INNER), F32),
                        pltpu.VMEM((SSD_GROUPS, SSD_STATE, GROUP_WIDTH), F32),
                        pltpu.VMEM((TM, SSD_INNER), BF16)],
        compiler_params=_cparams("parallel", "arbitrary"),
        name="ssd_branch",
    )(z_act, xbc_act, dt_raw, gates, dt_bias, a_log, d_skip, ssd_norm, w_ssd_out)


ROUTE_ROWS = 8
NT_DIMS = (((1,), (1,)), ((), ()))


def _first_row_of_max(vals, rowid, nrows):
    m = jnp.max(vals, axis=0, keepdims=True)
    idx = jnp.min(jnp.where(vals == m, rowid, nrows), axis=0, keepdims=True)
    return m, idx


def _merge_kernel(h_ref, yp_ref, ys_ref, wo_ref, nf_ref, wrh_ref, wrl_ref, br_ref,
                  h1_ref, hn_ref, route_ref, cnt_ref, carry, *, lp):
    i = pl.program_id(0)

    @pl.when(i == 0)
    def _():
        carry[...] = jnp.zeros_like(carry)

    merged = yp_ref[...].astype(F32) + ys_ref[...].astype(F32)
    h1 = h_ref[...] + jnp.dot(merged.astype(BF16), wo_ref[...], preferred_element_type=F32)
    h1_ref[...] = h1
    hn = h1 * lax.rsqrt(jnp.mean(h1 * h1, axis=-1, keepdims=True) + EPS) * nf_ref[...]
    _store_rows(hn_ref, hn)

    hn_hi = hn.astype(BF16)
    hn_lo = (hn - hn_hi.astype(F32)).astype(BF16)
    logits = (lax.dot_general(wrh_ref[...], hn_hi, NT_DIMS, preferred_element_type=F32)
              + lax.dot_general(wrl_ref[...], hn_hi, NT_DIMS, preferred_element_type=F32)
              + lax.dot_general(wrh_ref[...], hn_lo, NT_DIMS, preferred_element_type=F32)) + br_ref[...]
    gl = logits[0:N_EXPERT_GROUPS, :]
    grow = lax.broadcasted_iota(I32, (N_EXPERT_GROUPS, 1), 0)
    gmax, gidx = _first_row_of_max(gl, grow, N_EXPERT_GROUPS)
    g_p = 1.0 / jnp.sum(jnp.exp(gl - gmax), axis=0, keepdims=True)
    erow = lax.broadcasted_iota(I32, (N_EXPERTS, 1), 0)
    el_all = logits[N_EXPERT_GROUPS:N_EXPERT_GROUPS + N_EXPERTS, :]
    lo_row = gidx * EXPERTS_PER_GROUP
    el = jnp.where((erow >= lo_row) & (erow < lo_row + EXPERTS_PER_GROUP), el_all, -jnp.inf)
    m1, e1 = _first_row_of_max(el, erow, N_EXPERTS)
    m2, e2 = _first_row_of_max(jnp.where(erow == e1, -jnp.inf, el), erow, N_EXPERTS)
    t = jnp.exp(m2 - m1)
    w1 = g_p / (1.0 + t)
    w2 = g_p * t / (1.0 + t)

    coli = lax.broadcasted_iota(I32, (1, TM), 1)
    valid = (lax.rem(i * TM, lp) + coli) >= LEAD
    xrow = lax.broadcasted_iota(I32, (LANES, 1), 0)
    o1 = jnp.where((xrow == e1) & valid, 1.0, 0.0)
    o2 = jnp.where((xrow == e2) & valid, 1.0, 0.0)
    both = o1 + o2
    earlier = (lax.broadcasted_iota(I32, (TM, 1), 0) < coli).astype(BF16)
    before = jnp.dot(both.astype(BF16), earlier, preferred_element_type=F32) + carry[...]
    r1 = jnp.sum(o1 * before, axis=0, keepdims=True)
    r2 = jnp.sum(o2 * before, axis=0, keepdims=True)
    carry[...] = carry[...] + jnp.sum(both, axis=1, keepdims=True)
    cnt_ref[...] = jnp.broadcast_to(carry[...], cnt_ref.shape)

    zero = jnp.zeros_like(w1)
    route_ref[...] = jnp.concatenate(
        [e1.astype(F32), e2.astype(F32), r1, r2, w1, w2, zero, zero], axis=0)


def _merge_router(hp, yp, ys, w_out, norm_ffn, wr_hi, wr_lo, b_router, lp):
    rows = hp.shape[0]
    tile = lambda i: (i, 0)
    const = lambda i: (0, 0)
    return pl.pallas_call(
        functools.partial(_merge_kernel, lp=lp),
        grid=(rows // TM,),
        in_specs=[pl.BlockSpec((TM, D_MODEL), tile),
                  pl.BlockSpec((TM, D_MODEL), tile),
                  pl.BlockSpec((TM, D_MODEL), tile),
                  pl.BlockSpec((D_MODEL, D_MODEL), const),
                  pl.BlockSpec((1, D_MODEL), const),
                  pl.BlockSpec((LANES, D_MODEL), const),
                  pl.BlockSpec((LANES, D_MODEL), const),
                  pl.BlockSpec((LANES, TM), const)],
        out_specs=[pl.BlockSpec((TM, D_MODEL), tile),
                   pl.BlockSpec((TM * ROW_TILE, LANES), tile),
                   pl.BlockSpec((ROUTE_ROWS, TM), lambda i: (0, i)),
                   pl.BlockSpec((LANES, LANES), const)],
        out_shape=[jax.ShapeDtypeStruct((rows, D_MODEL), F32),
                   jax.ShapeDtypeStruct((rows * ROW_TILE, LANES), F32),
                   jax.ShapeDtypeStruct((ROUTE_ROWS, rows), F32),
                   jax.ShapeDtypeStruct((LANES, LANES), F32)],
        scratch_shapes=[pltpu.VMEM((LANES, 1), F32)],
        compiler_params=_cparams("arbitrary"),
        name="merge_router",
    )(hp, yp, ys, w_out, norm_ffn, wr_hi, wr_lo, b_router)


def _tile_rows(row, nrows=1):
    return pl.ds(pl.multiple_of(row * ROW_TILE, ROW_TILE), nrows * ROW_TILE)


def _row_copy(src, src_row, dst, dst_row, sem):
    return pltpu.make_async_copy(src.at[_tile_rows(src_row)], dst.at[_tile_rows(dst_row)], sem)


def _wait_rows(buf, nrows, sem):
    view = buf.at[pl.ds(0, nrows * ROW_TILE)]
    pltpu.make_async_copy(view, view, sem).wait()


def _load_rows(ref, nrows):
    return jnp.concatenate([ref[pl.ds(j, nrows, stride=ROW_TILE), :] for j in range(ROW_TILE)], axis=-1)


def _store_rows(ref, val):
    for j in range(ROW_TILE):
        ref[pl.ds(j, val.shape[0], stride=ROW_TILE), :] = val[:, j * LANES:(j + 1) * LANES]


def _dispatch_kernel(zb_ref, dest_ref, hn_ref, xs_out, zbuf, sem, zsem):
    first = pl.program_id(1) == 0

    @pl.when((pl.program_id(0) == 0) & first)
    def _():
        zbuf[...] = jnp.zeros_like(zbuf)

        def block_copy(j):
            return pltpu.make_async_copy(zbuf, xs_out.at[_tile_rows(jnp.maximum(zb_ref[j], 0) * MOE_BLOCK, MOE_BLOCK)], zsem)

        def start(j, carry):
            @pl.when(zb_ref[j] >= 0)
            def _():
                block_copy(j).start()
            return carry

        def finish(j, carry):
            @pl.when(zb_ref[j] >= 0)
            def _():
                block_copy(j).wait()
            return carry

        lax.fori_loop(0, ZERO_BLOCKS, start, 0)
        lax.fori_loop(0, ZERO_BLOCKS, finish, 0)

    def issue(blk, lo):
        def body(r, carry):
            for k in range(2):
                _row_copy(hn_ref, blk * CHUNK + r, xs_out, dest_ref[blk, k, r], sem).start()
            return carry
        lax.fori_loop(lo, CHUNK, body, 0, unroll=DMA_UNROLL)

    @pl.when(first)
    def _():
        issue(0, LEAD)

    @pl.when(jnp.logical_not(first))
    def _():
        issue(0, 0)

    for blk in range(1, CPT):
        issue(blk, 0)

    @pl.when(first)
    def _():
        for _k in range(2):
            _wait_rows(hn_ref, TM - LEAD, sem)

    @pl.when(jnp.logical_not(first))
    def _():
        for _k in range(2):
            _wait_rows(hn_ref, TM, sem)


def _dispatch(zero_blocks, dest, hn, nrows, bsz, lp):
    tiles = lp // TM
    return pl.pallas_call(
        _dispatch_kernel,
        grid_spec=pltpu.PrefetchScalarGridSpec(
            num_scalar_prefetch=1,
            grid=(bsz, tiles),
            in_specs=[pl.BlockSpec((CPT, 2, CHUNK), lambda b, c, zb: (b * tiles + c, 0, 0),
                                   memory_space=pltpu.SMEM),
                      pl.BlockSpec((TM * ROW_TILE, LANES), lambda b, c, zb: (b * tiles + c, 0))],
            out_specs=pl.BlockSpec(memory_space=pl.ANY),
            scratch_shapes=[pltpu.VMEM((MOE_BLOCK * ROW_TILE, LANES), F32),
                            pltpu.SemaphoreType.DMA(()),
                            pltpu.SemaphoreType.DMA(())]),
        out_shape=jax.ShapeDtypeStruct((nrows * ROW_TILE, LANES), F32),
        compiler_params=_cparams("arbitrary", "arbitrary"),
        name="moe_dispatch",
    )(zero_blocks, dest, hn)


def _expert_kernel(blk0_ref, nblk_ref, nused_ref, x_hbm, wgu_ref, wdn_ref, y_hbm,
                   xbuf, ybuf, wgu16, wdn16, xsem, ysem):
    e = pl.program_id(0)
    nblk = nblk_ref[e]
    nused = nused_ref[0]
    nb = y_hbm.shape[0] // (MOE_BLOCK * ROW_TILE)

    def x_copy(g, slot):
        return pltpu.make_async_copy(x_hbm.at[_tile_rows(g * MOE_BLOCK, MOE_BLOCK)], xbuf.at[slot], xsem.at[slot])

    def y_copy(g, slot):
        return pltpu.make_async_copy(ybuf.at[slot], y_hbm.at[_tile_rows(g * MOE_BLOCK, MOE_BLOCK)], ysem.at[slot])

    @pl.when((e == 0) & (nused > 0))
    def _():
        x_copy(0, 0).start()

    @pl.when(nblk > 0)
    def _():
        wgu16[...] = wgu_ref[0].astype(BF16)
        wdn16[...] = wdn_ref[0].astype(BF16)

        def body(j, carry):
            g = blk0_ref[e] + j
            slot = lax.rem(g, 2)

            @pl.when(g + 1 < nused)
            def _():
                x_copy(g + 1, 1 - slot).start()

            x_copy(g, slot).wait()

            @pl.when(g >= 2)
            def _():
                y_copy(g - 2, slot).wait()

            gu = jnp.dot(_load_rows(xbuf.at[slot], MOE_BLOCK).astype(BF16), wgu16[...],
                         preferred_element_type=F32)
            gate, up = gu[:, :D_EXPERT], gu[:, D_EXPERT:]
            act = gate * _sigmoid(gate) * up
            _store_rows(ybuf.at[slot], jnp.dot(act.astype(BF16), wdn16[...], preferred_element_type=F32))
            y_copy(g, slot).start()
            return carry

        lax.fori_loop(0, nblk, body, 0)

    @pl.when(e == pl.num_programs(0) - 1)
    def _():
        @pl.when(nused >= 2)
        def _():
            y_copy(nused - 2, lax.rem(nused, 2)).wait()

        @pl.when(nused >= 1)
        def _():
            y_copy(nused - 1, lax.rem(nused + 1, 2)).wait()

        ybuf[0] = jnp.zeros((MOE_BLOCK * ROW_TILE, LANES), F32)

        def zero_start(g, carry):
            y_copy(g, 0).start()
            return carry

        def zero_wait(g, carry):
            y_copy(g, 0).wait()
            return carry

        lax.fori_loop(nused, nb, zero_start, 0)
        lax.fori_loop(nused, nb, zero_wait, 0)


def _experts(blk0, nblk, nused, xs, w_gu, w_dn):
    wsel = lambda e, b0, nbk, nu: (e, 0, 0)
    return pl.pallas_call(
        _expert_kernel,
        grid_spec=pltpu.PrefetchScalarGridSpec(
            num_scalar_prefetch=3,
            grid=(N_EXPERTS,),
            in_specs=[pl.BlockSpec(memory_space=pl.ANY),
                      pl.BlockSpec((1, D_MODEL, 2 * D_EXPERT), wsel),
                      pl.BlockSpec((1, D_EXPERT, D_MODEL), wsel)],
            out_specs=pl.BlockSpec(memory_space=pl.ANY),
            scratch_shapes=[pltpu.VMEM((2, MOE_BLOCK * ROW_TILE, LANES), F32),
                            pltpu.VMEM((2, MOE_BLOCK * ROW_TILE, LANES), F32),
                            pltpu.VMEM((D_MODEL, 2 * D_EXPERT), BF16),
                            pltpu.VMEM((D_EXPERT, D_MODEL), BF16),
                            pltpu.SemaphoreType.DMA((2,)),
                            pltpu.SemaphoreType.DMA((2,))]),
        out_shape=jax.ShapeDtypeStruct(xs.shape, F32),
        compiler_params=_cparams("arbitrary"),
        name="moe_experts",
    )(blk0, nblk, nused, xs, w_gu, w_dn)


def _combine_kernel(da_ref, db_ref, dn_ref, h1a_ref, h1b_ref, rta_ref, rtb_ref, nf_ref, y_hbm,
                    o_ref, ya, yb, sem):
    step = pl.program_id(0) * pl.num_programs(1) + pl.program_id(1)
    nsteps = pl.num_programs(0) * pl.num_programs(1)

    def issue(d_ref, slot):
        def body(r, carry):
            _row_copy(y_hbm, d_ref[0, 0, r], ya.at[slot], r, sem.at[slot, 0]).start()
            _row_copy(y_hbm, d_ref[0, 1, r], yb.at[slot], r, sem.at[slot, 1]).start()
            return carry
        lax.fori_loop(0, CHUNK, body, 0, unroll=DMA_UNROLL)

    def finish(slot, h1_ref, rt_ref):
        _wait_rows(ya.at[slot], CHUNK, sem.at[slot, 0])
        _wait_rows(yb.at[slot], CHUNK, sem.at[slot, 1])
        rt = rt_ref[...]
        h2 = (h1_ref[...] + rt[:, 0:1] * _load_rows(ya.at[slot], CHUNK)
              + rt[:, 1:2] * _load_rows(yb.at[slot], CHUNK))
        o_ref[0, slot * CHUNK:(slot + 1) * CHUNK, :] = (
            h2 * lax.rsqrt(jnp.mean(h2 * h2, axis=-1, keepdims=True) + EPS) * nf_ref[...])

    @pl.when(step == 0)
    def _():
        issue(da_ref, 0)

    issue(db_ref, 1)
    finish(0, h1a_ref, rta_ref)

    @pl.when(step + 1 < nsteps)
    def _():
        issue(dn_ref, 0)

    finish(1, h1b_ref, rtb_ref)


def _combine(dest, h1, wts, norm_final, ys, bsz, lp, seq):
    nc = lp // CHUNK
    half = seq // (2 * CHUNK)
    first = lambda b, j: b * nc + 1 + 2 * j
    second = lambda b, j: b * nc + 2 + 2 * j

    def upcoming(b, j):
        nxt = jnp.minimum(b * half + j + 1, bsz * half - 1)
        return first(nxt // half, nxt % half)

    smem3 = lambda f: pl.BlockSpec((1, 2, CHUNK), lambda b, j: (f(b, j), 0, 0), memory_space=pltpu.SMEM)
    rows2 = lambda f, width: pl.BlockSpec((CHUNK, width), lambda b, j: (f(b, j), 0))
    return pl.pallas_call(
        _combine_kernel,
        grid=(bsz, half),
        in_specs=[smem3(first), smem3(second), smem3(upcoming),
                  rows2(first, D_MODEL), rows2(second, D_MODEL),
                  rows2(first, 2), rows2(second, 2),
                  pl.BlockSpec((1, D_MODEL), lambda b, j: (0, 0)),
                  pl.BlockSpec(memory_space=pl.ANY)],
        out_specs=pl.BlockSpec((1, 2 * CHUNK, D_MODEL), lambda b, j: (b, j, 0)),
        out_shape=jax.ShapeDtypeStruct((bsz, seq, D_MODEL), F32),
        scratch_shapes=[pltpu.VMEM((2, CHUNK * ROW_TILE, LANES), F32),
                        pltpu.VMEM((2, CHUNK * ROW_TILE, LANES), F32),
                        pltpu.SemaphoreType.DMA((2, 2))],
        compiler_params=_cparams("arbitrary", "arbitrary"),
        name="moe_combine",
    )(dest, dest, dest, h1, h1, wts, wts, norm_final, ys)


def _row(v):
    return v.reshape(1, -1).astype(F32)


def _pad_lanes(v):
    return jnp.pad(v, ((0, 0), (0, LANES - v.shape[1])))


def kernel(x, meta_tokens, norm_mix, w_in, pool_w, pool_scale, conv_w, conv_b, dt_bias, a_log, d_skip,
           ssd_norm, w_pool_out, w_ssd_out, w_out, norm_ffn, w_router_group, b_router_group,
           w_router_expert, b_router_expert, w_gate_up, w_down, norm_final):
    bsz, seq, _ = x.shape
    lp = LEAD + N_META + seq
    assert lp % TM == 0 and seq % (2 * CHUNK) == 0
    rows = bsz * lp

    meta = jnp.broadcast_to(meta_tokens.astype(x.dtype)[None], (bsz, N_META, D_MODEL))
    hp = jnp.concatenate([jnp.zeros((bsz, LEAD, D_MODEL), x.dtype), meta, x], axis=1)
    hp = hp.reshape(rows, D_MODEL)

    wi = w_in[0]
    o_z, o_x, o_dt, o_gp, o_gs = 1024, 3072, 6144, 6176, 7200
    cw = conv_w[0].astype(F32)
    cb = _row(conv_b[0])
    grp = lambda a, off, g, n: a[:, off + g * n:off + (g + 1) * n]
    w_groups, cw_groups, cb_groups = [], [], []
    for g in range(SSD_GROUPS):
        conv_cols = lambda a: jnp.concatenate(
            [grp(a, 0, g, GROUP_X), grp(a, SSD_INNER, g, SSD_STATE),
             grp(a, SSD_INNER + SSD_GROUPS * SSD_STATE, g, SSD_STATE)], axis=1)
        half_cols = jnp.concatenate([grp(wi, o_z, g, GROUP_Z), grp(wi, o_gp, g, GROUP_GATE // 2),
                                     grp(wi, o_gs, g, GROUP_GATE // 2)], axis=1)
        w_groups.append(jnp.concatenate([conv_cols(wi[:, o_x:o_dt]), 0.5 * half_cols,
                                         grp(wi, 0, g, GROUP_U)], axis=1).astype(BF16))
        cw_groups.append(0.5 * conv_cols(cw))
        cb_groups.append(0.5 * conv_cols(cb))
    xbc_act, z_act, gates, u, dt_raw = _projections(
        hp, _row(norm_mix[0]), jnp.stack(w_groups), jnp.stack(cw_groups), jnp.stack(cb_groups),
        _pad_lanes(wi[:, o_dt:o_gp]).astype(BF16), lp)

    yp = _pool_branch(u, gates, pool_w[0].astype(BF16), _row(pool_scale[0]),
                      w_pool_out[0].astype(BF16), lp)
    ys = _ssd_branch(z_act, xbc_act, dt_raw, gates,
                     _pad_lanes(_row(dt_bias[0])), _pad_lanes(_row(a_log[0])),
                     _row(jnp.repeat(d_skip[0], SSD_HEAD_DIM)), _row(ssd_norm[0]),
                     w_ssd_out[0].astype(BF16), bsz, lp)

    w_router = jnp.concatenate([w_router_group[0], w_router_expert[0]], axis=1).astype(F32).T
    w_router = jnp.pad(w_router, ((0, LANES - w_router.shape[0]), (0, 0)))
    wr_hi = w_router.astype(BF16)
    wr_lo = (w_router - wr_hi.astype(F32)).astype(BF16)
    b_router = _pad_lanes(jnp.concatenate([_row(b_router_group[0]), _row(b_router_expert[0])], axis=1))
    b_router = jnp.broadcast_to(b_router.reshape(LANES, 1), (LANES, TM))
    h1, hn, route, counts = _merge_router(hp, yp, ys, w_out[0].astype(BF16), _row(norm_ffn[0]),
                                          wr_hi, wr_lo, b_router, lp)

    n_assign = 2 * bsz * (N_META + seq)
    nb = -(-n_assign // MOE_BLOCK) + N_EXPERTS
    cnt = counts[:N_EXPERTS, 0].astype(I32)
    pcnt = (cnt + MOE_BLOCK - 1) // MOE_BLOCK * MOE_BLOCK
    pends = jnp.cumsum(pcnt)
    pstarts = pends - pcnt
    experts = route[0:2].astype(I32)
    eids = jnp.arange(N_EXPERTS, dtype=I32)[:, None]
    start_of = jnp.sum(jnp.where(experts[:, None, :] == eids, pstarts[:, None], 0), axis=1)
    dest = start_of + route[2:4].astype(I32)
    dest = dest.reshape(2, rows // CHUNK, CHUNK).transpose(1, 0, 2)
    wts = route[4:6].T
    nused = (pends[-1:] // MOE_BLOCK).astype(I32)
    last_blocks = jnp.where(pcnt > 0, pends // MOE_BLOCK - 1, -1)
    tail_ids = nused[0] + jnp.arange(ZERO_BLOCKS - N_EXPERTS, dtype=I32)
    zero_blocks = jnp.concatenate([last_blocks, jnp.where(tail_ids < nb, tail_ids, -1)]).astype(I32)

    xs = _dispatch(zero_blocks, dest, hn, nb * MOE_BLOCK, bsz, lp)
    yexp = _experts((pstarts // MOE_BLOCK).astype(I32), (pcnt // MOE_BLOCK).astype(I32), nused,
                    xs, w_gate_up[0], w_down[0])
    return _combine(dest, h1, wts, _row(norm_final), yexp, bsz, lp, seq)
```

```python
import functools
import math

import jax
import jax.numpy as jnp
from jax import lax
from jax.experimental import pallas as pl
from jax.experimental.pallas import tpu as pltpu

F32 = jnp.float32
BF16 = jnp.bfloat16
I32 = jnp.int32

D_MODEL = 1024
N_META = 16
EPS = 1e-6
LOG2E = math.log2(math.e)
CHUNK = 128
LEAD = CHUNK - N_META
CPT = 5
TM = CPT * CHUNK
POOL_WINDOWS = (2, 4, 8, 16)
POOL_GROUP = D_MODEL // len(POOL_WINDOWS)
POOL_HALO = 16
SSD_INNER = 2 * D_MODEL
SSD_HEADS = 32
SSD_HEAD_DIM = 64
SSD_GROUPS = 4
SSD_STATE = 128
SSD_CONV = 4
CONV_HALO = 8
BC_WIDTH = 2 * SSD_GROUPS * SSD_STATE
GROUP_WIDTH = SSD_INNER // SSD_GROUPS
PAIRS_PER_GROUP = GROUP_WIDTH // 128
SSD_LANES = 2
N_EXPERT_GROUPS = 8
EXPERTS_PER_GROUP = 8
N_EXPERTS = 64
D_EXPERT = 512
MOE_BLOCK = 512
LANES = 128
DMA_UNROLL = 8
ROW_TILE = D_MODEL // LANES
ZERO_BLOCKS = 2 * N_EXPERTS
STRIP = 256
GROUP_X = SSD_INNER // SSD_GROUPS
GROUP_CONV = GROUP_X + 2 * SSD_STATE
GROUP_Z = SSD_INNER // SSD_GROUPS
GROUP_GATE = 2 * D_MODEL // SSD_GROUPS
GROUP_U = D_MODEL // SSD_GROUPS
GROUP_COLS = GROUP_CONV + GROUP_Z + GROUP_GATE + GROUP_U

VMEM_LIMIT = 48 * 1024 * 1024


def _cparams(*sem):
    return pltpu.CompilerParams(dimension_semantics=sem, vmem_limit_bytes=VMEM_LIMIT)


def _sigmoid(x):
    return 1.0 / (1.0 + jnp.exp(-x))


def _silu_of_half(h):
    return h + h * jnp.tanh(h)


def _normed(h_ref, g_ref):
    x = h_ref[...]
    return (x * lax.rsqrt(jnp.mean(x * x, axis=-1, keepdims=True) + EPS) * g_ref[...]).astype(BF16)


def _proj_kernel(h_ref, g_ref, w_ref, cw_ref, cb_ref, wdt_ref,
                 xbc_ref, z_ref, gate_ref, u_ref, dt_ref, pbuf, *, lp):
    grp, i = pl.program_id(0), pl.program_id(1)

    @pl.when(i == 0)
    def _():
        pbuf[0:CONV_HALO, :] = jnp.zeros((CONV_HALO, GROUP_CONV), F32)

    hn = _normed(h_ref, g_ref)
    proj = lambda c0, n: jnp.dot(hn, w_ref[0, :, c0:c0 + n], preferred_element_type=F32)

    row = lax.rem(i * TM, lp) + lax.broadcasted_iota(I32, (CHUNK, 1), 0)
    lead_mask = row >= LEAD
    for c0 in range(0, GROUP_CONV, STRIP):
        cols = slice(c0, c0 + STRIP)
        pbuf[CONV_HALO:, cols] = proj(c0, STRIP)
        for r0 in range(0, TM, CHUNK):
            xa = pbuf[r0:r0 + CONV_HALO + CHUNK, cols]
            half = cb_ref[0, :, cols] + cw_ref[0, SSD_CONV - 1:SSD_CONV, cols] * xa[CONV_HALO:, :]
            for s in range(1, SSD_CONV):
                tap = cw_ref[0, SSD_CONV - 1 - s:SSD_CONV - s, cols]
                half = half + tap * pltpu.roll(xa, s, axis=0)[CONV_HALO:, :]
            act = _silu_of_half(half)
            if r0 == 0:
                act = jnp.where(lead_mask, act, 0.0)
            xbc_ref[0, r0:r0 + CHUNK, cols] = act.astype(BF16)
    pbuf[0:CONV_HALO, :] = pbuf[TM:TM + CONV_HALO, :]

    for c0 in range(0, GROUP_Z, STRIP):
        z_ref[0, :, c0:c0 + STRIP] = _silu_of_half(proj(GROUP_CONV + c0, STRIP)).astype(BF16)
    for c0 in range(0, GROUP_GATE, STRIP):
        half = proj(GROUP_CONV + GROUP_Z + c0, STRIP)
        gate_ref[0, :, c0:c0 + STRIP] = (0.5 + 0.5 * jnp.tanh(half)).astype(BF16)
    u_ref[0] = proj(GROUP_CONV + GROUP_Z + GROUP_GATE, GROUP_U)

    @pl.when(grp == 0)
    def _():
        dt_ref[...] = jnp.dot(hn, wdt_ref[...], preferred_element_type=F32)


def _projections(hp, gain, w_groups, cw_groups, cb_groups, w_dt, lp):
    rows = hp.shape[0]
    tiles = rows // TM
    per_group = lambda n: pl.BlockSpec((1, TM, n), lambda g, i: (g, i, 0))
    weights = lambda a: pl.BlockSpec((1,) + a.shape[1:], lambda g, i: (g, 0, 0))
    dt_spec = pl.BlockSpec((TM, LANES), lambda g, i: (jnp.where(g == 0, i, tiles - 1), 0))
    return pl.pallas_call(
        functools.partial(_proj_kernel, lp=lp),
        grid=(SSD_GROUPS, tiles),
        in_specs=[pl.BlockSpec((TM, D_MODEL), lambda g, i: (i, 0)),
                  pl.BlockSpec((1, D_MODEL), lambda g, i: (0, 0)),
                  weights(w_groups), weights(cw_groups), weights(cb_groups),
                  pl.BlockSpec(w_dt.shape, lambda g, i: (0, 0))],
        out_specs=[per_group(GROUP_CONV), per_group(GROUP_Z), per_group(GROUP_GATE), per_group(GROUP_U),
                   dt_spec],
        out_shape=[jax.ShapeDtypeStruct((SSD_GROUPS, rows, GROUP_CONV), BF16),
                   jax.ShapeDtypeStruct((SSD_GROUPS, rows, GROUP_Z), BF16),
                   jax.ShapeDtypeStruct((SSD_GROUPS, rows, GROUP_GATE), BF16),
                   jax.ShapeDtypeStruct((SSD_GROUPS, rows, GROUP_U), F32),
                   jax.ShapeDtypeStruct((rows, LANES), F32)],
        scratch_shapes=[pltpu.VMEM((CONV_HALO + TM, GROUP_CONV), F32)],
        compiler_params=_cparams("arbitrary", "arbitrary"),
        name="projections",
    )(hp, gain, w_groups, cw_groups, cb_groups, w_dt)


def _pool_kernel(u_ref, uprev_ref, gate_ref, pw_ref, ps_ref, wo_ref, o_ref, ubuf, *, lp):
    i = pl.program_id(0)
    ubuf[:, 0:POOL_HALO, :] = uprev_ref[...]
    ubuf[:, POOL_HALO:POOL_HALO + TM, :] = u_ref[...]
    row = lax.rem(i * TM, lp) + lax.broadcasted_iota(I32, (TM, 1), 0)
    tpos = row - LEAD
    parts = []
    for k, w in enumerate(POOL_WINDOWS):
        cur = ubuf[k, POOL_HALO:POOL_HALO + TM, :]
        acc = cur
        for s in range(1, w):
            acc = acc + ubuf[k, POOL_HALO - s:POOL_HALO - s + TM, :]
        cnt = jnp.clip(tpos + 1, 1, w).astype(F32)
        d = acc / cnt - cur
        parts.append(jnp.dot(d.astype(BF16), pw_ref[k], preferred_element_type=F32))
    pm = jnp.concatenate(parts, axis=-1) * ps_ref[...]
    y = jnp.dot(pm.astype(BF16), wo_ref[...], preferred_element_type=F32)
    gate = jnp.concatenate([gate_ref[k, :, 0:POOL_GROUP] for k in range(len(POOL_WINDOWS))], axis=-1)
    o_ref[...] = (gate.astype(F32) * y).astype(BF16)


def _pool_branch(u, gates, pool_w, pool_scale, w_pool_out, lp):
    ngroups, rows = u.shape[0], u.shape[1]
    halo_blocks = TM // POOL_HALO
    return pl.pallas_call(
        functools.partial(_pool_kernel, lp=lp),
        grid=(rows // TM,),
        in_specs=[pl.BlockSpec((ngroups, TM, POOL_GROUP), lambda i: (0, i, 0)),
                  pl.BlockSpec((ngroups, POOL_HALO, POOL_GROUP),
                               lambda i: (0, jnp.maximum(i * halo_blocks - 1, 0), 0)),
                  pl.BlockSpec((ngroups, TM, GROUP_GATE), lambda i: (0, i, 0)),
                  pl.BlockSpec((len(POOL_WINDOWS), POOL_GROUP, POOL_GROUP), lambda i: (0, 0, 0)),
                  pl.BlockSpec((1, D_MODEL), lambda i: (0, 0)),
                  pl.BlockSpec((D_MODEL, D_MODEL), lambda i: (0, 0))],
        out_specs=pl.BlockSpec((TM, D_MODEL), lambda i: (i, 0)),
        out_shape=jax.ShapeDtypeStruct((rows, D_MODEL), BF16),
        scratch_shapes=[pltpu.VMEM((ngroups, POOL_HALO + TM, POOL_GROUP), F32)],
        compiler_params=_cparams("parallel"),
        name="pool_branch",
    )(u, u, gates, pool_w, pool_scale, w_pool_out)


def _ssd_chunk(c, k, z_ref, xbc_ref, dt_ref, dtb_ref, alog_ref, dsk_ref, nrm_ref, y_ref, st_ref, yn_ref):
    rowi = lax.broadcasted_iota(I32, (CHUNK, 1), 0)
    coli = lax.broadcasted_iota(I32, (1, CHUNK), 1)
    causal = rowi >= coli
    left = coli < SSD_HEAD_DIM

    dtr = dt_ref[k] + dtb_ref[...]
    dt = jnp.maximum(dtr, 0.0) + jnp.log(1.0 + jnp.exp(-jnp.abs(dtr)))
    dt = jnp.where(rowi >= jnp.where(c == 0, LEAD, 0), dt, 0.0)
    adt2 = dt * (-LOG2E * jnp.exp(alog_ref[...]))
    a2 = jnp.dot(causal.astype(F32), adt2, precision=lax.Precision.HIGHEST,
                 preferred_element_type=F32)
    a2_t = a2.T
    dt_t = dt.T
    dtw_t = dt_t * jnp.exp2(a2_t[:, CHUNK - 1:CHUNK] - a2_t)

    for g in range(SSD_GROUPS):
        b_g = xbc_ref[g, k, :, GROUP_X:GROUP_X + SSD_STATE]
        c_g = xbc_ref[g, k, :, GROUP_X + SSD_STATE:GROUP_CONV]
        cb = lax.dot_general(c_g, b_g, (((1,), (1,)), ((), ())), preferred_element_type=F32)
        b_gt = b_g.T.astype(F32)
        y_off = jnp.dot(c_g, st_ref[g].astype(BF16), preferred_element_type=F32)
        for q in range(PAIRS_PER_GROUP):
            pair = g * PAIRS_PER_GROUP + q
            h0 = 2 * pair
            lanes = slice(pair * LANES, (pair + 1) * LANES)
            qcols = slice(q * LANES, (q + 1) * LANES)
            xq = xbc_ref[g, k, :, qcols]
            a_cols = [jnp.broadcast_to(a2[:, h:h + 1], (CHUNK, CHUNK)) for h in (h0, h0 + 1)]
            a_pair = jnp.where(left, a_cols[0], a_cols[1])
            y_heads, s_heads = [], []
            for j, h in enumerate((h0, h0 + 1)):
                seg = jnp.where(causal, a_cols[j] - a2_t[h:h + 1, :], -jnp.inf)
                m = (cb * jnp.exp2(seg) * dt_t[h:h + 1, :]).astype(BF16)
                y_heads.append(jnp.dot(m, xq, preferred_element_type=F32))
                bw = (b_gt * dtw_t[h:h + 1, :]).astype(BF16)
                s_heads.append(jnp.dot(bw, xq, preferred_element_type=F32))
            y_ref[:, lanes] = (jnp.where(left, y_heads[0], y_heads[1])
                               + y_off[:, qcols] * jnp.exp2(a_pair)
                               + dsk_ref[:, lanes] * xq.astype(F32))
            decay = jnp.exp2(a_pair[CHUNK - 1:CHUNK, :])
            st_ref[g, :, qcols] = st_ref[g, :, qcols] * decay + jnp.where(left, s_heads[0], s_heads[1])

    outs = []
    for g in range(SSD_GROUPS):
        yg = y_ref[:, g * GROUP_WIDTH:(g + 1) * GROUP_WIDTH] * z_ref[g, k].astype(F32)
        outs.append(yg * lax.rsqrt(jnp.mean(yg * yg, axis=-1, keepdims=True) + EPS))
    yn = jnp.concatenate(outs, axis=-1) * nrm_ref[...]
    slot = lax.rem(c, CPT)
    yn_ref[pl.ds(pl.multiple_of(slot * CHUNK, CHUNK), CHUNK), :] = yn.astype(BF16)


def _ssd_kernel(z_ref, xbc_ref, dt_ref, gate_ref, dtb_ref, alog_ref, dsk_ref, nrm_ref, wo_ref,
                o_ref, *scratch):
    c = pl.program_id(1)

    @pl.when(c == 0)
    def _():
        for k in range(SSD_LANES):
            scratch[3 * k + 1][...] = jnp.zeros_like(scratch[3 * k + 1])

    for k in range(SSD_LANES):
        y_ref, st_ref, yn_ref = scratch[3 * k:3 * k + 3]
        _ssd_chunk(c, k, z_ref, xbc_ref, dt_ref, dtb_ref, alog_ref, dsk_ref, nrm_ref, y_ref, st_ref, yn_ref)

    @pl.when(lax.rem(c, CPT) == CPT - 1)
    def _():
        for k in range(SSD_LANES):
            yo = jnp.dot(scratch[3 * k + 2][...], wo_ref[...], preferred_element_type=F32)
            gate = jnp.concatenate([gate_ref[j, k, :, GROUP_GATE // 2:] for j in range(SSD_GROUPS)], axis=-1)
            o_ref[k] = (gate.astype(F32) * yo).astype(BF16)


def _ssd_branch(z_act, xbc_act, dt_raw, gates, dt_bias, a_log, d_skip, ssd_norm, w_ssd_out, bsz, lp):
    nc = lp // CHUNK
    assert bsz % SSD_LANES == 0
    by_seq = lambda a: a.reshape(a.shape[:-2] + (bsz, lp, a.shape[-1]))
    grouped = lambda nrows, n, f: pl.BlockSpec((SSD_GROUPS, SSD_LANES, nrows, n),
                                               lambda b, c: (0, b, f(c), 0))
    chunk = lambda c: c
    tile = lambda c: c // CPT
    const = lambda b, c: (0, 0)
    out = pl.pallas_call(
        _ssd_kernel,
        grid=(bsz // SSD_LANES, nc),
        in_specs=[grouped(CHUNK, GROUP_Z, chunk),
                  grouped(CHUNK, GROUP_CONV, chunk),
                  pl.BlockSpec((SSD_LANES, CHUNK, LANES), lambda b, c: (b, c, 0)),
                  grouped(TM, GROUP_GATE, tile),
                  pl.BlockSpec((1, LANES), const),
                  pl.BlockSpec((1, LANES), const),
                  pl.BlockSpec((1, SSD_INNER), const),
                  pl.BlockSpec((1, SSD_INNER), const),
                  pl.BlockSpec((SSD_INNER, D_MODEL), const)],
        out_specs=pl.BlockSpec((SSD_LANES, TM, D_MODEL), lambda b, c: (b, c // CPT, 0)),
        out_shape=jax.ShapeDtypeStruct((bsz, lp, D_MODEL), BF16),
        scratch_shapes=[pltpu.VMEM((CHUNK, SSD_INNER), F32),
                        pltpu.VMEM((SSD_GROUPS, SSD_STATE, GROUP_WIDTH), F32),
                        pltpu.VMEM((TM, SSD_INNER), BF16)] * SSD_LANES,
        compiler_params=_cparams("parallel", "arbitrary"),
        name="ssd_branch",
    )(by_seq(z_act), by_seq(xbc_act), by_seq(dt_raw), by_seq(gates),
      dt_bias, a_log, d_skip, ssd_norm, w_ssd_out)
    return out.reshape(bsz * lp, D_MODEL)


ROUTE_ROWS = 8
NT_DIMS = (((1,), (1,)), ((), ()))


def _first_row_of_max(vals, rowid, nrows):
    m = jnp.max(vals, axis=0, keepdims=True)
    idx = jnp.min(jnp.where(vals == m, rowid, nrows), axis=0, keepdims=True)
    return m, idx


def _merge_kernel(h_ref, yp_ref, ys_ref, wo_ref, nf_ref, wrh_ref, wrl_ref, br_ref,
                  h1_ref, hn_ref, route_ref, cnt_ref, carry, *, lp):
    i = pl.program_id(0)

    @pl.when(i == 0)
    def _():
        carry[...] = jnp.zeros_like(carry)

    merged = yp_ref[...].astype(F32) + ys_ref[...].astype(F32)
    h1 = h_ref[...] + jnp.dot(merged.astype(BF16), wo_ref[...], preferred_element_type=F32)
    h1_ref[...] = h1
    hn = h1 * lax.rsqrt(jnp.mean(h1 * h1, axis=-1, keepdims=True) + EPS) * nf_ref[...]
    _store_rows(hn_ref, hn)

    hn_hi = hn.astype(BF16)
    hn_lo = (hn - hn_hi.astype(F32)).astype(BF16)
    logits = (lax.dot_general(wrh_ref[...], hn_hi, NT_DIMS, preferred_element_type=F32)
              + lax.dot_general(wrl_ref[...], hn_hi, NT_DIMS, preferred_element_type=F32)
              + lax.dot_general(wrh_ref[...], hn_lo, NT_DIMS, preferred_element_type=F32)) + br_ref[...]
    gl = logits[0:N_EXPERT_GROUPS, :]
    grow = lax.broadcasted_iota(I32, (N_EXPERT_GROUPS, 1), 0)
    gmax, gidx = _first_row_of_max(gl, grow, N_EXPERT_GROUPS)
    g_p = 1.0 / jnp.sum(jnp.exp(gl - gmax), axis=0, keepdims=True)
    erow = lax.broadcasted_iota(I32, (N_EXPERTS, 1), 0)
    el_all = logits[N_EXPERT_GROUPS:N_EXPERT_GROUPS + N_EXPERTS, :]
    lo_row = gidx * EXPERTS_PER_GROUP
    el = jnp.where((erow >= lo_row) & (erow < lo_row + EXPERTS_PER_GROUP), el_all, -jnp.inf)
    m1, e1 = _first_row_of_max(el, erow, N_EXPERTS)
    m2, e2 = _first_row_of_max(jnp.where(erow == e1, -jnp.inf, el), erow, N_EXPERTS)
    t = jnp.exp(m2 - m1)
    w1 = g_p / (1.0 + t)
    w2 = g_p * t / (1.0 + t)

    coli = lax.broadcasted_iota(I32, (1, TM), 1)
    valid = (lax.rem(i * TM, lp) + coli) >= LEAD
    xrow = lax.broadcasted_iota(I32, (LANES, 1), 0)
    o1 = jnp.where((xrow == e1) & valid, 1.0, 0.0)
    o2 = jnp.where((xrow == e2) & valid, 1.0, 0.0)
    both = o1 + o2
    earlier = (lax.broadcasted_iota(I32, (TM, 1), 0) < coli).astype(BF16)
    before = jnp.dot(both.astype(BF16), earlier, preferred_element_type=F32) + carry[...]
    r1 = jnp.sum(o1 * before, axis=0, keepdims=True)
    r2 = jnp.sum(o2 * before, axis=0, keepdims=True)
    carry[...] = carry[...] + jnp.sum(both, axis=1, keepdims=True)
    cnt_ref[...] = jnp.broadcast_to(carry[...], cnt_ref.shape)

    zero = jnp.zeros_like(w1)
    route_ref[...] = jnp.concatenate(
        [e1.astype(F32), e2.astype(F32), r1, r2, w1, w2, zero, zero], axis=0)


def _merge_router(hp, yp, ys, w_out, norm_ffn, wr_hi, wr_lo, b_router, lp):
    rows = hp.shape[0]
    tile = lambda i: (i, 0)
    const = lambda i: (0, 0)
    return pl.pallas_call(
        functools.partial(_merge_kernel, lp=lp),
        grid=(rows // TM,),
        in_specs=[pl.BlockSpec((TM, D_MODEL), tile),
                  pl.BlockSpec((TM, D_MODEL), tile),
                  pl.BlockSpec((TM, D_MODEL), tile),
                  pl.BlockSpec((D_MODEL, D_MODEL), const),
                  pl.BlockSpec((1, D_MODEL), const),
                  pl.BlockSpec((LANES, D_MODEL), const),
                  pl.BlockSpec((LANES, D_MODEL), const),
                  pl.BlockSpec((LANES, TM), const)],
        out_specs=[pl.BlockSpec((TM, D_MODEL), tile),
                   pl.BlockSpec((TM * ROW_TILE, LANES), tile),
                   pl.BlockSpec((ROUTE_ROWS, TM), lambda i: (0, i)),
                   pl.BlockSpec((LANES, LANES), const)],
        out_shape=[jax.ShapeDtypeStruct((rows, D_MODEL), F32),
                   jax.ShapeDtypeStruct((rows * ROW_TILE, LANES), F32),
                   jax.ShapeDtypeStruct((ROUTE_ROWS, rows), F32),
                   jax.ShapeDtypeStruct((LANES, LANES), F32)],
        scratch_shapes=[pltpu.VMEM((LANES, 1), F32)],
        compiler_params=_cparams("arbitrary"),
        name="merge_router",
    )(hp, yp, ys, w_out, norm_ffn, wr_hi, wr_lo, b_router)


def _tile_rows(row, nrows=1):
    return pl.ds(pl.multiple_of(row * ROW_TILE, ROW_TILE), nrows * ROW_TILE)


def _row_copy(src, src_row, dst, dst_row, sem):
    return pltpu.make_async_copy(src.at[_tile_rows(src_row)], dst.at[_tile_rows(dst_row)], sem)


def _wait_rows(buf, nrows, sem):
    view = buf.at[pl.ds(0, nrows * ROW_TILE)]
    pltpu.make_async_copy(view, view, sem).wait()


def _load_rows(ref, nrows):
    return jnp.concatenate([ref[pl.ds(j, nrows, stride=ROW_TILE), :] for j in range(ROW_TILE)], axis=-1)


def _store_rows(ref, val):
    for j in range(ROW_TILE):
        ref[pl.ds(j, val.shape[0], stride=ROW_TILE), :] = val[:, j * LANES:(j + 1) * LANES]


def _dispatch_kernel(zb_ref, dest_ref, hn_ref, xs_out, zbuf, sem, zsem):
    first = pl.program_id(1) == 0

    @pl.when((pl.program_id(0) == 0) & first)
    def _():
        zbuf[...] = jnp.zeros_like(zbuf)

        def block_copy(j):
            return pltpu.make_async_copy(zbuf, xs_out.at[_tile_rows(jnp.maximum(zb_ref[j], 0) * MOE_BLOCK, MOE_BLOCK)], zsem)

        def start(j, carry):
            @pl.when(zb_ref[j] >= 0)
            def _():
                block_copy(j).start()
            return carry

        def finish(j, carry):
            @pl.when(zb_ref[j] >= 0)
            def _():
                block_copy(j).wait()
            return carry

        lax.fori_loop(0, ZERO_BLOCKS, start, 0)
        lax.fori_loop(0, ZERO_BLOCKS, finish, 0)

    def issue(blk, lo):
        def body(r, carry):
            for k in range(2):
                _row_copy(hn_ref, blk * CHUNK + r, xs_out, dest_ref[blk, k, r], sem).start(priority=k)
            return carry
        lax.fori_loop(lo, CHUNK, body, 0, unroll=DMA_UNROLL)

    @pl.when(first)
    def _():
        issue(0, LEAD)

    @pl.when(jnp.logical_not(first))
    def _():
        issue(0, 0)

    for blk in range(1, CPT):
        issue(blk, 0)

    @pl.when(first)
    def _():
        for _k in range(2):
            _wait_rows(hn_ref, TM - LEAD, sem)

    @pl.when(jnp.logical_not(first))
    def _():
        for _k in range(2):
            _wait_rows(hn_ref, TM, sem)


def _dispatch(zero_blocks, dest, hn, nrows, bsz, lp):
    tiles = lp // TM
    return pl.pallas_call(
        _dispatch_kernel,
        grid_spec=pltpu.PrefetchScalarGridSpec(
            num_scalar_prefetch=1,
            grid=(bsz, tiles),
            in_specs=[pl.BlockSpec((CPT, 2, CHUNK), lambda b, c, zb: (b * tiles + c, 0, 0),
                                   memory_space=pltpu.SMEM),
                      pl.BlockSpec((TM * ROW_TILE, LANES), lambda b, c, zb: (b * tiles + c, 0))],
            out_specs=pl.BlockSpec(memory_space=pl.ANY),
            scratch_shapes=[pltpu.VMEM((MOE_BLOCK * ROW_TILE, LANES), F32),
                            pltpu.SemaphoreType.DMA(()),
                            pltpu.SemaphoreType.DMA(())]),
        out_shape=jax.ShapeDtypeStruct((nrows * ROW_TILE, LANES), F32),
        compiler_params=_cparams("arbitrary", "arbitrary"),
        name="moe_dispatch",
    )(zero_blocks, dest, hn)


def _expert_kernel(blk0_ref, nblk_ref, nused_ref, x_hbm, wgu_ref, wdn_ref, y_hbm,
                   xbuf, ybuf, wgu16, wdn16, xsem, ysem):
    e = pl.program_id(0)
    nblk = nblk_ref[e]
    nused = nused_ref[0]
    nb = y_hbm.shape[0] // (MOE_BLOCK * ROW_TILE)

    def x_copy(g, slot):
        return pltpu.make_async_copy(x_hbm.at[_tile_rows(g * MOE_BLOCK, MOE_BLOCK)], xbuf.at[slot], xsem.at[slot])

    def y_copy(g, slot):
        return pltpu.make_async_copy(ybuf.at[slot], y_hbm.at[_tile_rows(g * MOE_BLOCK, MOE_BLOCK)], ysem.at[slot])

    @pl.when((e == 0) & (nused > 0))
    def _():
        x_copy(0, 0).start()

    @pl.when(nblk > 0)
    def _():
        wgu16[...] = wgu_ref[0].astype(BF16)
        wdn16[...] = wdn_ref[0].astype(BF16)

        def body(j, carry):
            g = blk0_ref[e] + j
            slot = lax.rem(g, 2)

            @pl.when(g + 1 < nused)
            def _():
                x_copy(g + 1, 1 - slot).start()

            x_copy(g, slot).wait()

            @pl.when(g >= 2)
            def _():
                y_copy(g - 2, slot).wait()

            gu = jnp.dot(_load_rows(xbuf.at[slot], MOE_BLOCK).astype(BF16), wgu16[...],
                         preferred_element_type=F32)
            gate, up = gu[:, :D_EXPERT], gu[:, D_EXPERT:]
            act = gate * _sigmoid(gate) * up
            _store_rows(ybuf.at[slot], jnp.dot(act.astype(BF16), wdn16[...], preferred_element_type=F32))
            y_copy(g, slot).start()
            return carry

        lax.fori_loop(0, nblk, body, 0)

    @pl.when(e == pl.num_programs(0) - 1)
    def _():
        @pl.when(nused >= 2)
        def _():
            y_copy(nused - 2, lax.rem(nused, 2)).wait()

        @pl.when(nused >= 1)
        def _():
            y_copy(nused - 1, lax.rem(nused + 1, 2)).wait()

        ybuf[0] = jnp.zeros((MOE_BLOCK * ROW_TILE, LANES), F32)

        def zero_start(g, carry):
            y_copy(g, 0).start()
            return carry

        def zero_wait(g, carry):
            y_copy(g, 0).wait()
            return carry

        lax.fori_loop(nused, nb, zero_start, 0)
        lax.fori_loop(nused, nb, zero_wait, 0)


def _experts(blk0, nblk, nused, xs, w_gu, w_dn):
    wsel = lambda e, b0, nbk, nu: (e, 0, 0)
    return pl.pallas_call(
        _expert_kernel,
        grid_spec=pltpu.PrefetchScalarGridSpec(
            num_scalar_prefetch=3,
            grid=(N_EXPERTS,),
            in_specs=[pl.BlockSpec(memory_space=pl.ANY),
                      pl.BlockSpec((1, D_MODEL, 2 * D_EXPERT), wsel),
                      pl.BlockSpec((1, D_EXPERT, D_MODEL), wsel)],
            out_specs=pl.BlockSpec(memory_space=pl.ANY),
            scratch_shapes=[pltpu.VMEM((2, MOE_BLOCK * ROW_TILE, LANES), F32),
                            pltpu.VMEM((2, MOE_BLOCK * ROW_TILE, LANES), F32),
                            pltpu.VMEM((D_MODEL, 2 * D_EXPERT), BF16),
                            pltpu.VMEM((D_EXPERT, D_MODEL), BF16),
                            pltpu.SemaphoreType.DMA((2,)),
                            pltpu.SemaphoreType.DMA((2,))]),
        out_shape=jax.ShapeDtypeStruct(xs.shape, F32),
        compiler_params=_cparams("arbitrary"),
        name="moe_experts",
    )(blk0, nblk, nused, xs, w_gu, w_dn)


def _combine_kernel(da_ref, db_ref, dn_ref, h1a_ref, h1b_ref, rta_ref, rtb_ref, nf_ref, y_hbm,
                    o_ref, ya, yb, sem):
    step = pl.program_id(0) * pl.num_programs(1) + pl.program_id(1)
    nsteps = pl.num_programs(0) * pl.num_programs(1)

    def issue(d_ref, slot):
        def body(r, carry):
            _row_copy(y_hbm, d_ref[0, 0, r], ya.at[slot], r, sem.at[slot, 0]).start(priority=0)
            _row_copy(y_hbm, d_ref[0, 1, r], yb.at[slot], r, sem.at[slot, 1]).start(priority=1)
            return carry
        lax.fori_loop(0, CHUNK, body, 0, unroll=DMA_UNROLL)

    def finish(slot, h1_ref, rt_ref):
        _wait_rows(ya.at[slot], CHUNK, sem.at[slot, 0])
        _wait_rows(yb.at[slot], CHUNK, sem.at[slot, 1])
        rt = rt_ref[...]
        h2 = (h1_ref[...] + rt[:, 0:1] * _load_rows(ya.at[slot], CHUNK)
              + rt[:, 1:2] * _load_rows(yb.at[slot], CHUNK))
        o_ref[0, slot * CHUNK:(slot + 1) * CHUNK, :] = (
            h2 * lax.rsqrt(jnp.mean(h2 * h2, axis=-1, keepdims=True) + EPS) * nf_ref[...])

    @pl.when(step == 0)
    def _():
        issue(da_ref, 0)

    issue(db_ref, 1)
    finish(0, h1a_ref, rta_ref)

    @pl.when(step + 1 < nsteps)
    def _():
        issue(dn_ref, 0)

    finish(1, h1b_ref, rtb_ref)


def _combine(dest, h1, wts, norm_final, ys, bsz, lp, seq):
    nc = lp // CHUNK
    half = seq // (2 * CHUNK)
    first = lambda b, j: b * nc + 1 + 2 * j
    second = lambda b, j: b * nc + 2 + 2 * j

    def upcoming(b, j):
        nxt = jnp.minimum(b * half + j + 1, bsz * half - 1)
        return first(nxt // half, nxt % half)

    smem3 = lambda f: pl.BlockSpec((1, 2, CHUNK), lambda b, j: (f(b, j), 0, 0), memory_space=pltpu.SMEM)
    rows2 = lambda f, width: pl.BlockSpec((CHUNK, width), lambda b, j: (f(b, j), 0))
    return pl.pallas_call(
        _combine_kernel,
        grid=(bsz, half),
        in_specs=[smem3(first), smem3(second), smem3(upcoming),
                  rows2(first, D_MODEL), rows2(second, D_MODEL),
                  rows2(first, 2), rows2(second, 2),
                  pl.BlockSpec((1, D_MODEL), lambda b, j: (0, 0)),
                  pl.BlockSpec(memory_space=pl.ANY)],
        out_specs=pl.BlockSpec((1, 2 * CHUNK, D_MODEL), lambda b, j: (b, j, 0)),
        out_shape=jax.ShapeDtypeStruct((bsz, seq, D_MODEL), F32),
        scratch_shapes=[pltpu.VMEM((2, CHUNK * ROW_TILE, LANES), F32),
                        pltpu.VMEM((2, CHUNK * ROW_TILE, LANES), F32),
                        pltpu.SemaphoreType.DMA((2, 2))],
        compiler_params=_cparams("arbitrary", "arbitrary"),
        name="moe_combine",
    )(dest, dest, dest, h1, h1, wts, wts, norm_final, ys)


def _row(v):
    return v.reshape(1, -1).astype(F32)


def _pad_lanes(v):
    return jnp.pad(v, ((0, 0), (0, LANES - v.shape[1])))


def kernel(x, meta_tokens, norm_mix, w_in, pool_w, pool_scale, conv_w, conv_b, dt_bias, a_log, d_skip,
           ssd_norm, w_pool_out, w_ssd_out, w_out, norm_ffn, w_router_group, b_router_group,
           w_router_expert, b_router_expert, w_gate_up, w_down, norm_final):
    bsz, seq, _ = x.shape
    lp = LEAD + N_META + seq
    assert lp % TM == 0 and seq % (2 * CHUNK) == 0
    rows = bsz * lp

    meta = jnp.broadcast_to(meta_tokens.astype(x.dtype)[None], (bsz, N_META, D_MODEL))
    hp = jnp.concatenate([jnp.zeros((bsz, LEAD, D_MODEL), x.dtype), meta, x], axis=1)
    hp = hp.reshape(rows, D_MODEL)

    wi = w_in[0]
    o_z, o_x, o_dt, o_gp, o_gs = 1024, 3072, 6144, 6176, 7200
    cw = conv_w[0].astype(F32)
    cb = _row(conv_b[0])
    grp = lambda a, off, g, n: a[:, off + g * n:off + (g + 1) * n]
    w_groups, cw_groups, cb_groups = [], [], []
    for g in range(SSD_GROUPS):
        conv_cols = lambda a: jnp.concatenate(
            [grp(a, 0, g, GROUP_X), grp(a, SSD_INNER, g, SSD_STATE),
             grp(a, SSD_INNER + SSD_GROUPS * SSD_STATE, g, SSD_STATE)], axis=1)
        half_cols = jnp.concatenate([grp(wi, o_z, g, GROUP_Z), grp(wi, o_gp, g, GROUP_GATE // 2),
                                     grp(wi, o_gs, g, GROUP_GATE // 2)], axis=1)
        w_groups.append(jnp.concatenate([conv_cols(wi[:, o_x:o_dt]), 0.5 * half_cols,
                                         grp(wi, 0, g, GROUP_U)], axis=1).astype(BF16))
        cw_groups.append(0.5 * conv_cols(cw))
        cb_groups.append(0.5 * conv_cols(cb))
    xbc_act, z_act, gates, u, dt_raw = _projections(
        hp, _row(norm_mix[0]), jnp.stack(w_groups), jnp.stack(cw_groups), jnp.stack(cb_groups),
        _pad_lanes(wi[:, o_dt:o_gp]).astype(BF16), lp)

    yp = _pool_branch(u, gates, pool_w[0].astype(BF16), _row(pool_scale[0]),
                      w_pool_out[0].astype(BF16), lp)
    ys = _ssd_branch(z_act, xbc_act, dt_raw, gates,
                     _pad_lanes(_row(dt_bias[0])), _pad_lanes(_row(a_log[0])),
                     _row(jnp.repeat(d_skip[0], SSD_HEAD_DIM)), _row(ssd_norm[0]),
                     w_ssd_out[0].astype(BF16), bsz, lp)

    w_router = jnp.concatenate([w_router_group[0], w_router_expert[0]], axis=1).astype(F32).T
    w_router = jnp.pad(w_router, ((0, LANES - w_router.shape[0]), (0, 0)))
    wr_hi = w_router.astype(BF16)
    wr_lo = (w_router - wr_hi.astype(F32)).astype(BF16)
    b_router = _pad_lanes(jnp.concatenate([_row(b_router_group[0]), _row(b_router_expert[0])], axis=1))
    b_router = jnp.broadcast_to(b_router.reshape(LANES, 1), (LANES, TM))
    h1, hn, route, counts = _merge_router(hp, yp, ys, w_out[0].astype(BF16), _row(norm_ffn[0]),
                                          wr_hi, wr_lo, b_router, lp)

    n_assign = 2 * bsz * (N_META + seq)
    nb = -(-n_assign // MOE_BLOCK) + N_EXPERTS
    cnt = counts[:N_EXPERTS, 0].astype(I32)
    pcnt = (cnt + MOE_BLOCK - 1) // MOE_BLOCK * MOE_BLOCK
    pends = jnp.cumsum(pcnt)
    pstarts = pends - pcnt
    experts = route[0:2].astype(I32)
    eids = jnp.arange(N_EXPERTS, dtype=I32)[:, None]
    start_of = jnp.sum(jnp.where(experts[:, None, :] == eids, pstarts[:, None], 0), axis=1)
    dest = start_of + route[2:4].astype(I32)
    dest = dest.reshape(2, rows // CHUNK, CHUNK).transpose(1, 0, 2)
    wts = route[4:6].T
    nused = (pends[-1:] // MOE_BLOCK).astype(I32)
    last_blocks = jnp.where(pcnt > 0, pends // MOE_BLOCK - 1, -1)
    tail_ids = nused[0] + jnp.arange(ZERO_BLOCKS - N_EXPERTS, dtype=I32)
    zero_blocks = jnp.concatenate([last_blocks, jnp.where(tail_ids < nb, tail_ids, -1)]).astype(I32)

    xs = _dispatch(zero_blocks, dest, hn, nb * MOE_BLOCK, bsz, lp)
    yexp = _experts((pstarts // MOE_BLOCK).astype(I32), (pcnt // MOE_BLOCK).astype(I32), nused,
                    xs, w_gate_up[0], w_down[0])
    return _combine(dest, h1, wts, _row(norm_final), yexp, bsz, lp, seq)
```

```python
import functools
import math

import jax
import jax.numpy as jnp
from jax import lax
from jax.experimental import pallas as pl
from jax.experimental.pallas import tpu as pltpu

F32 = jnp.float32
BF16 = jnp.bfloat16
I32 = jnp.int32

D_MODEL = 1024
N_META = 16
EPS = 1e-6
LOG2E = math.log2(math.e)
CHUNK = 128
LEAD = CHUNK - N_META
CPT = 5
TM = CPT * CHUNK
POOL_WINDOWS = (2, 4, 8, 16)
POOL_GROUP = D_MODEL // len(POOL_WINDOWS)
POOL_HALO = 16
SSD_INNER = 2 * D_MODEL
SSD_HEADS = 32
SSD_HEAD_DIM = 64
SSD_GROUPS = 4
SSD_STATE = 128
SSD_CONV = 4
CONV_HALO = 8
BC_WIDTH = 2 * SSD_GROUPS * SSD_STATE
GROUP_WIDTH = SSD_INNER // SSD_GROUPS
PAIRS_PER_GROUP = GROUP_WIDTH // 128
SSD_LANES = 2
N_EXPERT_GROUPS = 8
EXPERTS_PER_GROUP = 8
N_EXPERTS = 64
D_EXPERT = 512
MOE_BLOCK = 512
LANES = 128
DMA_UNROLL = 8
ROW_TILE = D_MODEL // LANES
ZERO_BLOCKS = 2 * N_EXPERTS
STRIP = 256
GROUP_X = SSD_INNER // SSD_GROUPS
GROUP_CONV = GROUP_X + 2 * SSD_STATE
GROUP_Z = SSD_INNER // SSD_GROUPS
GROUP_GATE = 2 * D_MODEL // SSD_GROUPS
GROUP_U = D_MODEL // SSD_GROUPS
GROUP_COLS = GROUP_CONV + GROUP_Z + GROUP_GATE + GROUP_U

VMEM_LIMIT = 48 * 1024 * 1024


def _cparams(*sem):
    return pltpu.CompilerParams(dimension_semantics=sem, vmem_limit_bytes=VMEM_LIMIT)


def _sigmoid(x):
    return 1.0 / (1.0 + jnp.exp(-x))


def _silu_of_half(h):
    return h + h * jnp.tanh(h)


def _normed(x, g_ref):
    return (x * lax.rsqrt(jnp.mean(x * x, axis=-1, keepdims=True) + EPS) * g_ref[...]).astype(BF16)


def _row_tile_specs(tiles_per_seq, tile_of):
    def chunk_spec(j):
        def index(*ids):
            tile = tile_of(*ids)
            return tile // tiles_per_seq, jnp.maximum(lax.rem(tile, tiles_per_seq) * CPT - 1 + j, 0), 0
        return pl.BlockSpec((1, CHUNK, D_MODEL), index)
    return [chunk_spec(j) for j in range(CPT)] + [pl.BlockSpec((N_META, D_MODEL), lambda *ids: (0, 0))]


def _row_tile(refs, first_tile_of_seq):
    chunk_refs, meta_ref = refs[:CPT], refs[CPT]
    lead = jnp.concatenate([jnp.zeros((LEAD, D_MODEL), F32), meta_ref[...]], axis=0)
    first = jnp.where(first_tile_of_seq, lead, chunk_refs[0][0])
    return jnp.concatenate([first] + [r[0] for r in chunk_refs[1:]], axis=0)


def _proj_kernel(*refs, lp):
    h_refs, (g_ref, w_ref, cw_ref, cb_ref, wdt_ref,
             xbc_ref, z_ref, gate_ref, u_ref, dt_ref, pbuf) = refs[:CPT + 1], refs[CPT + 1:]
    grp, i = pl.program_id(0), pl.program_id(1)

    @pl.when(i == 0)
    def _():
        pbuf[0:CONV_HALO, :] = jnp.zeros((CONV_HALO, GROUP_CONV), F32)

    hn = _normed(_row_tile(h_refs, lax.rem(i * TM, lp) == 0), g_ref)
    proj = lambda c0, n: jnp.dot(hn, w_ref[0, :, c0:c0 + n], preferred_element_type=F32)

    row = lax.rem(i * TM, lp) + lax.broadcasted_iota(I32, (CHUNK, 1), 0)
    lead_mask = row >= LEAD
    for c0 in range(0, GROUP_CONV, STRIP):
        cols = slice(c0, c0 + STRIP)
        pbuf[CONV_HALO:, cols] = proj(c0, STRIP)
        for r0 in range(0, TM, CHUNK):
            xa = pbuf[r0:r0 + CONV_HALO + CHUNK, cols]
            half = cb_ref[0, :, cols] + cw_ref[0, SSD_CONV - 1:SSD_CONV, cols] * xa[CONV_HALO:, :]
            for s in range(1, SSD_CONV):
                tap = cw_ref[0, SSD_CONV - 1 - s:SSD_CONV - s, cols]
                half = half + tap * pltpu.roll(xa, s, axis=0)[CONV_HALO:, :]
            act = _silu_of_half(half)
            if r0 == 0:
                act = jnp.where(lead_mask, act, 0.0)
            xbc_ref[0, r0:r0 + CHUNK, cols] = act.astype(BF16)
    pbuf[0:CONV_HALO, :] = pbuf[TM:TM + CONV_HALO, :]

    for c0 in range(0, GROUP_Z, STRIP):
        z_ref[0, :, c0:c0 + STRIP] = _silu_of_half(proj(GROUP_CONV + c0, STRIP)).astype(BF16)
    for c0 in range(0, GROUP_GATE, STRIP):
        half = proj(GROUP_CONV + GROUP_Z + c0, STRIP)
        gate_ref[0, :, c0:c0 + STRIP] = (0.5 + 0.5 * jnp.tanh(half)).astype(BF16)
    u_ref[0] = proj(GROUP_CONV + GROUP_Z + GROUP_GATE, GROUP_U)

    @pl.when(grp == 0)
    def _():
        dt_ref[...] = jnp.dot(hn, wdt_ref[...], preferred_element_type=F32)


def _projections(x, meta, gain, w_groups, cw_groups, cb_groups, w_dt, lp):
    rows = x.shape[0] * lp
    tiles = rows // TM
    per_group = lambda n: pl.BlockSpec((1, TM, n), lambda g, i: (g, i, 0))
    weights = lambda a: pl.BlockSpec((1,) + a.shape[1:], lambda g, i: (g, 0, 0))
    dt_spec = pl.BlockSpec((TM, LANES), lambda g, i: (jnp.where(g == 0, i, tiles - 1), 0))
    return pl.pallas_call(
        functools.partial(_proj_kernel, lp=lp),
        grid=(SSD_GROUPS, tiles),
        in_specs=_row_tile_specs(lp // TM, lambda g, i: i)
        + [pl.BlockSpec((1, D_MODEL), lambda g, i: (0, 0)),
           weights(w_groups), weights(cw_groups), weights(cb_groups),
           pl.BlockSpec(w_dt.shape, lambda g, i: (0, 0))],
        out_specs=[per_group(GROUP_CONV), per_group(GROUP_Z), per_group(GROUP_GATE), per_group(GROUP_U),
                   dt_spec],
        out_shape=[jax.ShapeDtypeStruct((SSD_GROUPS, rows, GROUP_CONV), BF16),
                   jax.ShapeDtypeStruct((SSD_GROUPS, rows, GROUP_Z), BF16),
                   jax.ShapeDtypeStruct((SSD_GROUPS, rows, GROUP_GATE), BF16),
                   jax.ShapeDtypeStruct((SSD_GROUPS, rows, GROUP_U), F32),
                   jax.ShapeDtypeStruct((rows, LANES), F32)],
        scratch_shapes=[pltpu.VMEM((CONV_HALO + TM, GROUP_CONV), F32)],
        compiler_params=_cparams("arbitrary", "arbitrary"),
        name="projections",
    )(*([x] * CPT), meta, gain, w_groups, cw_groups, cb_groups, w_dt)


def _pool_mixer(u_ref, uprev_ref, pw_ref, ps_ref, wo_ref, ubuf, tpos):
    ubuf[:, 0:POOL_HALO, :] = uprev_ref[...]
    ubuf[:, POOL_HALO:POOL_HALO + TM, :] = u_ref[...]
    parts = []
    for k, w in enumerate(POOL_WINDOWS):
        cur = ubuf[k, POOL_HALO:POOL_HALO + TM, :]
        acc = cur
        for s in range(1, w):
            acc = acc + ubuf[k, POOL_HALO - s:POOL_HALO - s + TM, :]
        cnt = jnp.clip(tpos + 1, 1, w).astype(F32)
        d = acc / cnt - cur
        parts.append(jnp.dot(d.astype(BF16), pw_ref[k], preferred_element_type=F32))
    pm = jnp.concatenate(parts, axis=-1) * ps_ref[...]
    return jnp.dot(pm.astype(BF16), wo_ref[...], preferred_element_type=F32)


def _ssd_chunk(c, k, z_ref, xbc_ref, dt_ref, dtb_ref, alog_ref, dsk_ref, nrm_ref, y_ref, st_ref, yn_ref):
    rowi = lax.broadcasted_iota(I32, (CHUNK, 1), 0)
    coli = lax.broadcasted_iota(I32, (1, CHUNK), 1)
    causal = rowi >= coli
    left = coli < SSD_HEAD_DIM

    dtr = dt_ref[k] + dtb_ref[...]
    dt = jnp.maximum(dtr, 0.0) + jnp.log(1.0 + jnp.exp(-jnp.abs(dtr)))
    dt = jnp.where(rowi >= jnp.where(c == 0, LEAD, 0), dt, 0.0)
    adt2 = dt * (-LOG2E * jnp.exp(alog_ref[...]))
    a2 = jnp.dot(causal.astype(F32), adt2, precision=lax.Precision.HIGHEST,
                 preferred_element_type=F32)
    a2_t = a2.T
    dt_t = dt.T
    dtw_t = dt_t * jnp.exp2(a2_t[:, CHUNK - 1:CHUNK] - a2_t)

    for g in range(SSD_GROUPS):
        b_g = xbc_ref[g, k, :, GROUP_X:GROUP_X + SSD_STATE]
        c_g = xbc_ref[g, k, :, GROUP_X + SSD_STATE:GROUP_CONV]
        cb = lax.dot_general(c_g, b_g, (((1,), (1,)), ((), ())), preferred_element_type=F32)
        b_gt = b_g.T.astype(F32)
        y_off = jnp.dot(c_g, st_ref[g].astype(BF16), preferred_element_type=F32)
        for q in range(PAIRS_PER_GROUP):
            pair = g * PAIRS_PER_GROUP + q
            h0 = 2 * pair
            lanes = slice(pair * LANES, (pair + 1) * LANES)
            qcols = slice(q * LANES, (q + 1) * LANES)
            xq = xbc_ref[g, k, :, qcols]
            a_cols = [jnp.broadcast_to(a2[:, h:h + 1], (CHUNK, CHUNK)) for h in (h0, h0 + 1)]
            a_pair = jnp.where(left, a_cols[0], a_cols[1])
            y_heads, s_heads = [], []
            for j, h in enumerate((h0, h0 + 1)):
                seg = jnp.where(causal, a_cols[j] - a2_t[h:h + 1, :], -jnp.inf)
                m = (cb * jnp.exp2(seg) * dt_t[h:h + 1, :]).astype(BF16)
                y_heads.append(jnp.dot(m, xq, preferred_element_type=F32))
                bw = (b_gt * dtw_t[h:h + 1, :]).astype(BF16)
                s_heads.append(jnp.dot(bw, xq, preferred_element_type=F32))
            y_ref[:, lanes] = (jnp.where(left, y_heads[0], y_heads[1])
                               + y_off[:, qcols] * jnp.exp2(a_pair)
                               + dsk_ref[:, lanes] * xq.astype(F32))
            decay = jnp.exp2(a_pair[CHUNK - 1:CHUNK, :])
            st_ref[g, :, qcols] = st_ref[g, :, qcols] * decay + jnp.where(left, s_heads[0], s_heads[1])

    outs = []
    for g in range(SSD_GROUPS):
        yg = y_ref[:, g * GROUP_WIDTH:(g + 1) * GROUP_WIDTH] * z_ref[g, k].astype(F32)
        outs.append(yg * lax.rsqrt(jnp.mean(yg * yg, axis=-1, keepdims=True) + EPS))
    yn = jnp.concatenate(outs, axis=-1) * nrm_ref[...]
    slot = lax.rem(c, CPT)
    yn_ref[pl.ds(pl.multiple_of(slot * CHUNK, CHUNK), CHUNK), :] = yn.astype(BF16)


def _ssd_kernel(z_ref, xbc_ref, dt_ref, gate_ref, dtb_ref, alog_ref, dsk_ref, nrm_ref, wo_ref,
                o_ref, *scratch):
    c = pl.program_id(1)

    @pl.when(c == 0)
    def _():
        for k in range(SSD_LANES):
            scratch[3 * k + 1][...] = jnp.zeros_like(scratch[3 * k + 1])

    for k in range(SSD_LANES):
        y_ref, st_ref, yn_ref = scratch[3 * k:3 * k + 3]
        _ssd_chunk(c, k, z_ref, xbc_ref, dt_ref, dtb_ref, alog_ref, dsk_ref, nrm_ref, y_ref, st_ref, yn_ref)

    @pl.when(lax.rem(c, CPT) == CPT - 1)
    def _():
        for k in range(SSD_LANES):
            yo = jnp.dot(scratch[3 * k + 2][...], wo_ref[...], preferred_element_type=F32)
            gate = jnp.concatenate([gate_ref[j, k, :, GROUP_GATE // 2:] for j in range(SSD_GROUPS)], axis=-1)
            o_ref[k] = (gate.astype(F32) * yo).astype(BF16)


def _ssd_branch(z_act, xbc_act, dt_raw, gates, dt_bias, a_log, d_skip, ssd_norm, w_ssd_out, bsz, lp):
    nc = lp // CHUNK
    assert bsz % SSD_LANES == 0
    by_seq = lambda a: a.reshape(a.shape[:-2] + (bsz, lp, a.shape[-1]))
    grouped = lambda nrows, n, f: pl.BlockSpec((SSD_GROUPS, SSD_LANES, nrows, n),
                                               lambda b, c: (0, b, f(c), 0))
    chunk = lambda c: c
    tile = lambda c: c // CPT
    const = lambda b, c: (0, 0)
    out = pl.pallas_call(
        _ssd_kernel,
        grid=(bsz // SSD_LANES, nc),
        in_specs=[grouped(CHUNK, GROUP_Z, chunk),
                  grouped(CHUNK, GROUP_CONV, chunk),
                  pl.BlockSpec((SSD_LANES, CHUNK, LANES), lambda b, c: (b, c, 0)),
                  grouped(TM, GROUP_GATE, tile),
                  pl.BlockSpec((1, LANES), const),
                  pl.BlockSpec((1, LANES), const),
                  pl.BlockSpec((1, SSD_INNER), const),
                  pl.BlockSpec((1, SSD_INNER), const),
                  pl.BlockSpec((SSD_INNER, D_MODEL), const)],
        out_specs=pl.BlockSpec((SSD_LANES, TM, D_MODEL), lambda b, c: (b, c // CPT, 0)),
        out_shape=jax.ShapeDtypeStruct((bsz, lp, D_MODEL), BF16),
        scratch_shapes=[pltpu.VMEM((CHUNK, SSD_INNER), F32),
                        pltpu.VMEM((SSD_GROUPS, SSD_STATE, GROUP_WIDTH), F32),
                        pltpu.VMEM((TM, SSD_INNER), BF16)] * SSD_LANES,
        compiler_params=_cparams("parallel", "arbitrary"),
        name="ssd_branch",
    )(by_seq(z_act), by_seq(xbc_act), by_seq(dt_raw), by_seq(gates),
      dt_bias, a_log, d_skip, ssd_norm, w_ssd_out)
    return out.reshape(bsz * lp, D_MODEL)


ROUTE_ROWS = 8
NT_DIMS = (((1,), (1,)), ((), ()))


def _first_row_of_max(vals, rowid, nrows):
    m = jnp.max(vals, axis=0, keepdims=True)
    idx = jnp.min(jnp.where(vals == m, rowid, nrows), axis=0, keepdims=True)
    return m, idx


def _merge_kernel(*refs, lp):
    h_refs, (u_ref, uprev_ref, gate_ref, ys_ref, pw_ref, ps_ref, wpo_ref,
             wo_ref, nf_ref, wrh_ref, wrl_ref, br_ref,
             h1_ref, hn_ref, route_ref, cnt_ref, carry, ubuf) = refs[:CPT + 1], refs[CPT + 1:]
    i = pl.program_id(0)

    @pl.when(i == 0)
    def _():
        carry[...] = jnp.zeros_like(carry)

    tile_row0 = lax.rem(i * TM, lp)
    tpos = tile_row0 + lax.broadcasted_iota(I32, (TM, 1), 0) - LEAD
    y_pool = _pool_mixer(u_ref, uprev_ref, pw_ref, ps_ref, wpo_ref, ubuf, tpos)
    g_pool = jnp.concatenate([gate_ref[k, :, 0:POOL_GROUP] for k in range(len(POOL_WINDOWS))], axis=-1)
    merged = g_pool.astype(F32) * y_pool + ys_ref[...].astype(F32)
    h1 = _row_tile(h_refs, tile_row0 == 0) + jnp.dot(merged.astype(BF16), wo_ref[...],
                                                     preferred_element_type=F32)
    h1_ref[...] = h1
    hn = h1 * lax.rsqrt(jnp.mean(h1 * h1, axis=-1, keepdims=True) + EPS) * nf_ref[...]
    _store_rows(hn_ref, hn)

    hn_hi = hn.astype(BF16)
    hn_lo = (hn - hn_hi.astype(F32)).astype(BF16)
    logits = (lax.dot_general(wrh_ref[...], hn_hi, NT_DIMS, preferred_element_type=F32)
              + lax.dot_general(wrl_ref[...], hn_hi, NT_DIMS, preferred_element_type=F32)
              + lax.dot_general(wrh_ref[...], hn_lo, NT_DIMS, preferred_element_type=F32)) + br_ref[...]
    gl = logits[0:N_EXPERT_GROUPS, :]
    grow = lax.broadcasted_iota(I32, (N_EXPERT_GROUPS, 1), 0)
    gmax, gidx = _first_row_of_max(gl, grow, N_EXPERT_GROUPS)
    g_p = 1.0 / jnp.sum(jnp.exp(gl - gmax), axis=0, keepdims=True)
    erow = lax.broadcasted_iota(I32, (N_EXPERTS, 1), 0)
    el_all = logits[N_EXPERT_GROUPS:N_EXPERT_GROUPS + N_EXPERTS, :]
    lo_row = gidx * EXPERTS_PER_GROUP
    el = jnp.where((erow >= lo_row) & (erow < lo_row + EXPERTS_PER_GROUP), el_all, -jnp.inf)
    m1, e1 = _first_row_of_max(el, erow, N_EXPERTS)
    m2, e2 = _first_row_of_max(jnp.where(erow == e1, -jnp.inf, el), erow, N_EXPERTS)
    t = jnp.exp(m2 - m1)
    w1 = g_p / (1.0 + t)
    w2 = g_p * t / (1.0 + t)

    coli = lax.broadcasted_iota(I32, (1, TM), 1)
    valid = (tile_row0 + coli) >= LEAD
    xrow = lax.broadcasted_iota(I32, (LANES, 1), 0)
    o1 = jnp.where((xrow == e1) & valid, 1.0, 0.0)
    o2 = jnp.where((xrow == e2) & valid, 1.0, 0.0)
    both = o1 + o2
    earlier = (lax.broadcasted_iota(I32, (TM, 1), 0) < coli).astype(BF16)
    before = jnp.dot(both.astype(BF16), earlier, preferred_element_type=F32) + carry[...]
    r1 = jnp.sum(o1 * before, axis=0, keepdims=True)
    r2 = jnp.sum(o2 * before, axis=0, keepdims=True)
    carry[...] = carry[...] + jnp.sum(both, axis=1, keepdims=True)
    cnt_ref[...] = jnp.broadcast_to(carry[...], cnt_ref.shape)

    zero = jnp.zeros_like(w1)
    route_ref[...] = jnp.concatenate(
        [e1.astype(F32), e2.astype(F32), r1, r2, w1, w2, zero, zero], axis=0)


def _merge_router(x, meta, u, gates, ys, pool_w, pool_scale, w_pool_out, w_out, norm_ffn,
                  wr_hi, wr_lo, b_router, lp):
    rows = x.shape[0] * lp
    ngroups = u.shape[0]
    halo_blocks = TM // POOL_HALO
    tile = lambda i: (i, 0)
    const = lambda i: (0, 0)
    fixed = lambda a: pl.BlockSpec(a.shape, lambda i: (0,) * a.ndim, pipeline_mode=pl.Buffered(1))
    return pl.pallas_call(
        functools.partial(_merge_kernel, lp=lp),
        grid=(rows // TM,),
        in_specs=_row_tile_specs(lp // TM, lambda i: i)
        + [pl.BlockSpec((ngroups, TM, POOL_GROUP), lambda i: (0, i, 0)),
           pl.BlockSpec((ngroups, POOL_HALO, POOL_GROUP),
                        lambda i: (0, jnp.maximum(i * halo_blocks - 1, 0), 0)),
           pl.BlockSpec((ngroups, TM, GROUP_GATE), lambda i: (0, i, 0)),
           pl.BlockSpec((TM, D_MODEL), tile),
           fixed(pool_w), fixed(pool_scale), fixed(w_pool_out), fixed(w_out), fixed(norm_ffn),
           fixed(wr_hi), fixed(wr_lo), fixed(b_router)],
        out_specs=[pl.BlockSpec((TM, D_MODEL), tile),
                   pl.BlockSpec((TM * ROW_TILE, LANES), tile),
                   pl.BlockSpec((ROUTE_ROWS, TM), lambda i: (0, i)),
                   pl.BlockSpec((LANES, LANES), const)],
        out_shape=[jax.ShapeDtypeStruct((rows, D_MODEL), F32),
                   jax.ShapeDtypeStruct((rows * ROW_TILE, LANES), F32),
                   jax.ShapeDtypeStruct((ROUTE_ROWS, rows), F32),
                   jax.ShapeDtypeStruct((LANES, LANES), F32)],
        scratch_shapes=[pltpu.VMEM((LANES, 1), F32),
                        pltpu.VMEM((ngroups, POOL_HALO + TM, POOL_GROUP), F32)],
        compiler_params=_cparams("arbitrary"),
        name="merge_router",
    )(*([x] * CPT), meta, u, u, gates, ys, pool_w, pool_scale, w_pool_out, w_out, norm_ffn,
      wr_hi, wr_lo, b_router)


def _tile_rows(row, nrows=1):
    return pl.ds(pl.multiple_of(row * ROW_TILE, ROW_TILE), nrows * ROW_TILE)


def _row_copy(src, src_row, dst, dst_row, sem):
    return pltpu.make_async_copy(src.at[_tile_rows(src_row)], dst.at[_tile_rows(dst_row)], sem)


def _wait_rows(buf, nrows, sem):
    view = buf.at[pl.ds(0, nrows * ROW_TILE)]
    pltpu.make_async_copy(view, view, sem).wait()


def _load_rows(ref, nrows):
    return jnp.concatenate([ref[pl.ds(j, nrows, stride=ROW_TILE), :] for j in range(ROW_TILE)], axis=-1)


def _store_rows(ref, val):
    for j in range(ROW_TILE):
        ref[pl.ds(j, val.shape[0], stride=ROW_TILE), :] = val[:, j * LANES:(j + 1) * LANES]


def _dispatch_kernel(zb_ref, dest_ref, hn_ref, xs_out, zbuf, sem, zsem):
    first = pl.program_id(1) == 0

    @pl.when((pl.program_id(0) == 0) & first)
    def _():
        zbuf[...] = jnp.zeros_like(zbuf)

        def block_copy(j):
            return pltpu.make_async_copy(zbuf, xs_out.at[_tile_rows(jnp.maximum(zb_ref[j], 0) * MOE_BLOCK, MOE_BLOCK)], zsem)

        def start(j, carry):
            @pl.when(zb_ref[j] >= 0)
            def _():
                block_copy(j).start()
            return carry

        def finish(j, carry):
            @pl.when(zb_ref[j] >= 0)
            def _():
                block_copy(j).wait()
            return carry

        lax.fori_loop(0, ZERO_BLOCKS, start, 0)
        lax.fori_loop(0, ZERO_BLOCKS, finish, 0)

    def issue(blk, lo):
        def body(r, carry):
            for k in range(2):
                _row_copy(hn_ref, blk * CHUNK + r, xs_out, dest_ref[blk, k, r], sem).start(priority=k)
            return carry
        lax.fori_loop(lo, CHUNK, body, 0, unroll=DMA_UNROLL)

    @pl.when(first)
    def _():
        issue(0, LEAD)

    @pl.when(jnp.logical_not(first))
    def _():
        issue(0, 0)

    for blk in range(1, CPT):
        issue(blk, 0)

    @pl.when(first)
    def _():
        for _k in range(2):
            _wait_rows(hn_ref, TM - LEAD, sem)

    @pl.when(jnp.logical_not(first))
    def _():
        for _k in range(2):
            _wait_rows(hn_ref, TM, sem)


def _dispatch(zero_blocks, dest, hn, nrows, bsz, lp):
    tiles = lp // TM
    return pl.pallas_call(
        _dispatch_kernel,
        grid_spec=pltpu.PrefetchScalarGridSpec(
            num_scalar_prefetch=1,
            grid=(bsz, tiles),
            in_specs=[pl.BlockSpec((CPT, 2, CHUNK), lambda b, c, zb: (b * tiles + c, 0, 0),
                                   memory_space=pltpu.SMEM),
                      pl.BlockSpec((TM * ROW_TILE, LANES), lambda b, c, zb: (b * tiles + c, 0))],
            out_specs=pl.BlockSpec(memory_space=pl.ANY),
            scratch_shapes=[pltpu.VMEM((MOE_BLOCK * ROW_TILE, LANES), F32),
                            pltpu.SemaphoreType.DMA(()),
                            pltpu.SemaphoreType.DMA(())]),
        out_shape=jax.ShapeDtypeStruct((nrows * ROW_TILE, LANES), F32),
        compiler_params=_cparams("arbitrary", "arbitrary"),
        name="moe_dispatch",
    )(zero_blocks, dest, hn)


def _expert_kernel(blk0_ref, nblk_ref, nused_ref, x_hbm, wgu_ref, wdn_ref, y_hbm,
                   xbuf, ybuf, wgu16, wdn16, xsem, ysem):
    e = pl.program_id(0)
    nblk = nblk_ref[e]
    nused = nused_ref[0]
    nb = y_hbm.shape[0] // (MOE_BLOCK * ROW_TILE)

    def x_copy(g, slot):
        return pltpu.make_async_copy(x_hbm.at[_tile_rows(g * MOE_BLOCK, MOE_BLOCK)], xbuf.at[slot], xsem.at[slot])

    def y_copy(g, slot):
        return pltpu.make_async_copy(ybuf.at[slot], y_hbm.at[_tile_rows(g * MOE_BLOCK, MOE_BLOCK)], ysem.at[slot])

    @pl.when((e == 0) & (nused > 0))
    def _():
        x_copy(0, 0).start()

    @pl.when(nblk > 0)
    def _():
        wgu16[...] = wgu_ref[0].astype(BF16)
        wdn16[...] = wdn_ref[0].astype(BF16)

        def body(j, carry):
            g = blk0_ref[e] + j
            slot = lax.rem(g, 2)

            @pl.when(g + 1 < nused)
            def _():
                x_copy(g + 1, 1 - slot).start()

            x_copy(g, slot).wait()

            @pl.when(g >= 2)
            def _():
                y_copy(g - 2, slot).wait()

            gu = jnp.dot(_load_rows(xbuf.at[slot], MOE_BLOCK).astype(BF16), wgu16[...],
                         preferred_element_type=F32)
            gate, up = gu[:, :D_EXPERT], gu[:, D_EXPERT:]
            act = gate * _sigmoid(gate) * up
            _store_rows(ybuf.at[slot], jnp.dot(act.astype(BF16), wdn16[...], preferred_element_type=F32))
            y_copy(g, slot).start()
            return carry

        lax.fori_loop(0, nblk, body, 0)

    @pl.when(e == pl.num_programs(0) - 1)
    def _():
        @pl.when(nused >= 2)
        def _():
            y_copy(nused - 2, lax.rem(nused, 2)).wait()

        @pl.when(nused >= 1)
        def _():
            y_copy(nused - 1, lax.rem(nused + 1, 2)).wait()

        ybuf[0] = jnp.zeros((MOE_BLOCK * ROW_TILE, LANES), F32)

        def zero_start(g, carry):
            y_copy(g, 0).start()
            return carry

        def zero_wait(g, carry):
            y_copy(g, 0).wait()
            return carry

        lax.fori_loop(nused, nb, zero_start, 0)
        lax.fori_loop(nused, nb, zero_wait, 0)


def _experts(blk0, nblk, nused, xs, w_gu, w_dn):
    wsel = lambda e, b0, nbk, nu: (e, 0, 0)
    return pl.pallas_call(
        _expert_kernel,
        grid_spec=pltpu.PrefetchScalarGridSpec(
            num_scalar_prefetch=3,
            grid=(N_EXPERTS,),
            in_specs=[pl.BlockSpec(memory_space=pl.ANY),
                      pl.BlockSpec((1, D_MODEL, 2 * D_EXPERT), wsel),
                      pl.BlockSpec((1, D_EXPERT, D_MODEL), wsel)],
            out_specs=pl.BlockSpec(memory_space=pl.ANY),
            scratch_shapes=[pltpu.VMEM((2, MOE_BLOCK * ROW_TILE, LANES), F32),
                            pltpu.VMEM((2, MOE_BLOCK * ROW_TILE, LANES), F32),
                            pltpu.VMEM((D_MODEL, 2 * D_EXPERT), BF16),
                            pltpu.VMEM((D_EXPERT, D_MODEL), BF16),
                            pltpu.SemaphoreType.DMA((2,)),
                            pltpu.SemaphoreType.DMA((2,))]),
        out_shape=jax.ShapeDtypeStruct(xs.shape, F32),
        compiler_params=_cparams("arbitrary"),
        name="moe_experts",
    )(blk0, nblk, nused, xs, w_gu, w_dn)


def _combine_kernel(da_ref, db_ref, dn_ref, h1a_ref, h1b_ref, rta_ref, rtb_ref, nf_ref, y_hbm,
                    o_ref, ya, yb, sem):
    step = pl.program_id(0) * pl.num_programs(1) + pl.program_id(1)
    nsteps = pl.num_programs(0) * pl.num_programs(1)

    def issue(d_ref, slot):
        def body(r, carry):
            _row_copy(y_hbm, d_ref[0, 0, r], ya.at[slot], r, sem.at[slot, 0]).start(priority=0)
            _row_copy(y_hbm, d_ref[0, 1, r], yb.at[slot], r, sem.at[slot, 1]).start(priority=1)
            return carry
        lax.fori_loop(0, CHUNK, body, 0, unroll=DMA_UNROLL)

    def finish(slot, h1_ref, rt_ref):
        _wait_rows(ya.at[slot], CHUNK, sem.at[slot, 0])
        _wait_rows(yb.at[slot], CHUNK, sem.at[slot, 1])
        rt = rt_ref[...]
        h2 = (h1_ref[...] + rt[:, 0:1] * _load_rows(ya.at[slot], CHUNK)
              + rt[:, 1:2] * _load_rows(yb.at[slot], CHUNK))
        o_ref[0, slot * CHUNK:(slot + 1) * CHUNK, :] = (
            h2 * lax.rsqrt(jnp.mean(h2 * h2, axis=-1, keepdims=True) + EPS) * nf_ref[...])

    @pl.when(step == 0)
    def _():
        issue(da_ref, 0)

    issue(db_ref, 1)
    finish(0, h1a_ref, rta_ref)

    @pl.when(step + 1 < nsteps)
    def _():
        issue(dn_ref, 0)

    finish(1, h1b_ref, rtb_ref)


def _combine(dest, h1, wts, norm_final, ys, bsz, lp, seq):
    nc = lp // CHUNK
    half = seq // (2 * CHUNK)
    first = lambda b, j: b * nc + 1 + 2 * j
    second = lambda b, j: b * nc + 2 + 2 * j

    def upcoming(b, j):
        nxt = jnp.minimum(b * half + j + 1, bsz * half - 1)
        return first(nxt // half, nxt % half)

    smem3 = lambda f: pl.BlockSpec((1, 2, CHUNK), lambda b, j: (f(b, j), 0, 0), memory_space=pltpu.SMEM)
    rows2 = lambda f, width: pl.BlockSpec((CHUNK, width), lambda b, j: (f(b, j), 0))
    return pl.pallas_call(
        _combine_kernel,
        grid=(bsz, half),
        in_specs=[smem3(first), smem3(second), smem3(upcoming),
                  rows2(first, D_MODEL), rows2(second, D_MODEL),
                  rows2(first, 2), rows2(second, 2),
                  pl.BlockSpec((1, D_MODEL), lambda b, j: (0, 0)),
                  pl.BlockSpec(memory_space=pl.ANY)],
        out_specs=pl.BlockSpec((1, 2 * CHUNK, D_MODEL), lambda b, j: (b, j, 0)),
        out_shape=jax.ShapeDtypeStruct((bsz, seq, D_MODEL), F32),
        scratch_shapes=[pltpu.VMEM((2, CHUNK * ROW_TILE, LANES), F32),
                        pltpu.VMEM((2, CHUNK * ROW_TILE, LANES), F32),
                        pltpu.SemaphoreType.DMA((2, 2))],
        compiler_params=_cparams("arbitrary", "arbitrary"),
        name="moe_combine",
    )(dest, dest, dest, h1, h1, wts, wts, norm_final, ys)


def _row(v):
    return v.reshape(1, -1).astype(F32)


def _pad_lanes(v):
    return jnp.pad(v, ((0, 0), (0, LANES - v.shape[1])))


def kernel(x, meta_tokens, norm_mix, w_in, pool_w, pool_scale, conv_w, conv_b, dt_bias, a_log, d_skip,
           ssd_norm, w_pool_out, w_ssd_out, w_out, norm_ffn, w_router_group, b_router_group,
           w_router_expert, b_router_expert, w_gate_up, w_down, norm_final):
    bsz, seq, _ = x.shape
    lp = LEAD + N_META + seq
    assert lp % TM == 0 and seq % (2 * CHUNK) == 0
    rows = bsz * lp

    meta = meta_tokens.astype(x.dtype)

    wi = w_in[0]
    o_z, o_x, o_dt, o_gp, o_gs = 1024, 3072, 6144, 6176, 7200
    cw = conv_w[0].astype(F32)
    cb = _row(conv_b[0])
    grp = lambda a, off, g, n: a[:, off + g * n:off + (g + 1) * n]
    w_groups, cw_groups, cb_groups = [], [], []
    for g in range(SSD_GROUPS):
        conv_cols = lambda a: jnp.concatenate(
            [grp(a, 0, g, GROUP_X), grp(a, SSD_INNER, g, SSD_STATE),
             grp(a, SSD_INNER + SSD_GROUPS * SSD_STATE, g, SSD_STATE)], axis=1)
        half_cols = jnp.concatenate([grp(wi, o_z, g, GROUP_Z), grp(wi, o_gp, g, GROUP_GATE // 2),
                                     grp(wi, o_gs, g, GROUP_GATE // 2)], axis=1)
        w_groups.append(jnp.concatenate([conv_cols(wi[:, o_x:o_dt]), 0.5 * half_cols,
                                         grp(wi, 0, g, GROUP_U)], axis=1).astype(BF16))
        cw_groups.append(0.5 * conv_cols(cw))
        cb_groups.append(0.5 * conv_cols(cb))
    xbc_act, z_act, gates, u, dt_raw = _projections(
        x, meta, _row(norm_mix[0]), jnp.stack(w_groups), jnp.stack(cw_groups), jnp.stack(cb_groups),
        _pad_lanes(wi[:, o_dt:o_gp]).astype(BF16), lp)

    ys = _ssd_branch(z_act, xbc_act, dt_raw, gates,
                     _pad_lanes(_row(dt_bias[0])), _pad_lanes(_row(a_log[0])),
                     _row(jnp.repeat(d_skip[0], SSD_HEAD_DIM)), _row(ssd_norm[0]),
                     w_ssd_out[0].astype(BF16), bsz, lp)

    w_router = jnp.concatenate([w_router_group[0], w_router_expert[0]], axis=1).astype(F32).T
    w_router = jnp.pad(w_router, ((0, LANES - w_router.shape[0]), (0, 0)))
    wr_hi = w_router.astype(BF16)
    wr_lo = (w_router - wr_hi.astype(F32)).astype(BF16)
    b_router = _pad_lanes(jnp.concatenate([_row(b_router_group[0]), _row(b_router_expert[0])], axis=1))
    b_router = jnp.broadcast_to(b_router.reshape(LANES, 1), (LANES, TM))
    h1, hn, route, counts = _merge_router(x, meta, u, gates, ys, pool_w[0].astype(BF16), _row(pool_scale[0]),
                                          w_pool_out[0].astype(BF16), w_out[0].astype(BF16),
                                          _row(norm_ffn[0]), wr_hi, wr_lo, b_router, lp)

    n_assign = 2 * bsz * (N_META + seq)
    nb = -(-n_assign // MOE_BLOCK) + N_EXPERTS
    cnt = counts[:N_EXPERTS, 0].astype(I32)
    pcnt = (cnt + MOE_BLOCK - 1) // MOE_BLOCK * MOE_BLOCK
    pends = jnp.cumsum(pcnt)
    pstarts = pends - pcnt
    experts = route[0:2].astype(I32)
    eids = jnp.arange(N_EXPERTS, dtype=I32)[:, None]
    start_of = jnp.sum(jnp.where(experts[:, None, :] == eids, pstarts[:, None], 0), axis=1)
    dest = start_of + route[2:4].astype(I32)
    dest = dest.reshape(2, rows // CHUNK, CHUNK).transpose(1, 0, 2)
    wts = route[4:6].T
    nused = (pends[-1:] // MOE_BLOCK).astype(I32)
    last_blocks = jnp.where(pcnt > 0, pends // MOE_BLOCK - 1, -1)
    tail_ids = nused[0] + jnp.arange(ZERO_BLOCKS - N_EXPERTS, dtype=I32)
    zero_blocks = jnp.concatenate([last_blocks, jnp.where(tail_ids < nb, tail_ids, -1)]).astype(I32)

    xs = _dispatch(zero_blocks, dest, hn, nb * MOE_BLOCK, bsz, lp)
    yexp = _experts((pstarts // MOE_BLOCK).astype(I32), (pcnt // MOE_BLOCK).astype(I32), nused,
                    xs, w_gate_up[0], w_down[0])
    return _combine(dest, h1, wts, _row(norm_final), yexp, bsz, lp, seq)
```

```python
import functools
import math

import jax
import jax.numpy as jnp
from jax import lax
from jax.experimental import pallas as pl
from jax.experimental.pallas import tpu as pltpu

F32 = jnp.float32
BF16 = jnp.bfloat16
I32 = jnp.int32

D_MODEL = 1024
N_META = 16
EPS = 1e-6
LOG2E = math.log2(math.e)
CHUNK = 128
LEAD = CHUNK - N_META
CPT = 5
TM = CPT * CHUNK
POOL_WINDOWS = (2, 4, 8, 16)
POOL_GROUP = D_MODEL // len(POOL_WINDOWS)
POOL_HALO = 16
SSD_INNER = 2 * D_MODEL
SSD_HEADS = 32
SSD_HEAD_DIM = 64
SSD_GROUPS = 4
SSD_STATE = 128
SSD_CONV = 4
CONV_HALO = 8
BC_WIDTH = 2 * SSD_GROUPS * SSD_STATE
GROUP_WIDTH = SSD_INNER // SSD_GROUPS
PAIRS_PER_GROUP = GROUP_WIDTH // 128
SSD_LANES = 2
N_EXPERT_GROUPS = 8
EXPERTS_PER_GROUP = 8
N_EXPERTS = 64
D_EXPERT = 512
MOE_BLOCK = 512
LANES = 128
DMA_UNROLL = 8
ROW_TILE = D_MODEL // LANES
ZERO_BLOCKS = 2 * N_EXPERTS
STRIP = 256
GROUP_X = SSD_INNER // SSD_GROUPS
GROUP_CONV = GROUP_X + 2 * SSD_STATE
GROUP_Z = SSD_INNER // SSD_GROUPS
GROUP_GATE = 2 * D_MODEL // SSD_GROUPS
GROUP_U = D_MODEL // SSD_GROUPS
GROUP_COLS = GROUP_CONV + GROUP_Z + GROUP_GATE + GROUP_U

VMEM_LIMIT = 48 * 1024 * 1024


def _cparams(*sem):
    return pltpu.CompilerParams(dimension_semantics=sem, vmem_limit_bytes=VMEM_LIMIT)


def _sigmoid(x):
    return 1.0 / (1.0 + jnp.exp(-x))


def _silu_of_half(h):
    return h + h * jnp.tanh(h)


def _normed(x, g_ref):
    return (x * lax.rsqrt(jnp.mean(x * x, axis=-1, keepdims=True) + EPS) * g_ref[...]).astype(BF16)


def _row_tile_specs(tiles_per_seq, tile_of):
    def chunk_spec(j):
        def index(*ids):
            tile = tile_of(*ids)
            return tile // tiles_per_seq, jnp.maximum(lax.rem(tile, tiles_per_seq) * CPT - 1 + j, 0), 0
        return pl.BlockSpec((1, CHUNK, D_MODEL), index)
    return [chunk_spec(j) for j in range(CPT)] + [pl.BlockSpec((N_META, D_MODEL), lambda *ids: (0, 0))]


def _row_tile(refs, first_tile_of_seq):
    chunk_refs, meta_ref = refs[:CPT], refs[CPT]
    lead = jnp.concatenate([jnp.zeros((LEAD, D_MODEL), F32), meta_ref[...]], axis=0)
    first = jnp.where(first_tile_of_seq, lead, chunk_refs[0][0])
    return jnp.concatenate([first] + [r[0] for r in chunk_refs[1:]], axis=0)


def _proj_kernel(*refs, lp):
    h_refs, (g_ref, w_ref, cw_ref, cb_ref, wdt_ref,
             xbc_ref, z_ref, gate_ref, u_ref, dt_ref, hn_ref, pbuf, tails) = refs[:CPT + 1], refs[CPT + 1:]
    i, grp = pl.program_id(0), pl.program_id(1)

    @pl.when(grp == 0)
    def _():
        hn_ref[...] = _normed(_row_tile(h_refs, lax.rem(i * TM, lp) == 0), g_ref)
        dt_ref[...] = jnp.dot(hn_ref[...], wdt_ref[...], preferred_element_type=F32)

    @pl.when(i == 0)
    def _():
        tails[grp] = jnp.zeros((CONV_HALO, GROUP_CONV), F32)

    pbuf[0:CONV_HALO, :] = tails[grp]
    hn = hn_ref[...]
    proj = lambda c0, n: jnp.dot(hn, w_ref[0, :, c0:c0 + n], preferred_element_type=F32)

    row = lax.rem(i * TM, lp) + lax.broadcasted_iota(I32, (CHUNK, 1), 0)
    lead_mask = row >= LEAD
    for c0 in range(0, GROUP_CONV, STRIP):
        cols = slice(c0, c0 + STRIP)
        pbuf[CONV_HALO:, cols] = proj(c0, STRIP)
        for r0 in range(0, TM, CHUNK):
            xa = pbuf[r0:r0 + CONV_HALO + CHUNK, cols]
            half = cb_ref[0, :, cols] + cw_ref[0, SSD_CONV - 1:SSD_CONV, cols] * xa[CONV_HALO:, :]
            for s in range(1, SSD_CONV):
                tap = cw_ref[0, SSD_CONV - 1 - s:SSD_CONV - s, cols]
                half = half + tap * pltpu.roll(xa, s, axis=0)[CONV_HALO:, :]
            act = _silu_of_half(half)
            if r0 == 0:
                act = jnp.where(lead_mask, act, 0.0)
            xbc_ref[0, r0:r0 + CHUNK, cols] = act.astype(BF16)
    tails[grp] = pbuf[TM:TM + CONV_HALO, :]

    for c0 in range(0, GROUP_Z, STRIP):
        z_ref[0, :, c0:c0 + STRIP] = _silu_of_half(proj(GROUP_CONV + c0, STRIP)).astype(BF16)
    for c0 in range(0, GROUP_GATE, STRIP):
        half = proj(GROUP_CONV + GROUP_Z + c0, STRIP)
        gate_ref[0, :, c0:c0 + STRIP] = (0.5 + 0.5 * jnp.tanh(half)).astype(BF16)
    u_ref[0] = proj(GROUP_CONV + GROUP_Z + GROUP_GATE, GROUP_U)


def _projections(x, meta, gain, w_groups, cw_groups, cb_groups, w_dt, lp):
    rows = x.shape[0] * lp
    tiles = rows // TM
    per_group = lambda n: pl.BlockSpec((1, TM, n), lambda i, g: (g, i, 0))
    weights = lambda a: pl.BlockSpec((1,) + a.shape[1:], lambda i, g: (g, 0, 0))
    return pl.pallas_call(
        functools.partial(_proj_kernel, lp=lp),
        grid=(tiles, SSD_GROUPS),
        in_specs=_row_tile_specs(lp // TM, lambda i, g: i)
        + [pl.BlockSpec((1, D_MODEL), lambda i, g: (0, 0)),
           weights(w_groups), weights(cw_groups), weights(cb_groups),
           pl.BlockSpec(w_dt.shape, lambda i, g: (0, 0))],
        out_specs=[per_group(GROUP_CONV), per_group(GROUP_Z), per_group(GROUP_GATE), per_group(GROUP_U),
                   pl.BlockSpec((TM, LANES), lambda i, g: (i, 0))],
        out_shape=[jax.ShapeDtypeStruct((SSD_GROUPS, rows, GROUP_CONV), BF16),
                   jax.ShapeDtypeStruct((SSD_GROUPS, rows, GROUP_Z), BF16),
                   jax.ShapeDtypeStruct((SSD_GROUPS, rows, GROUP_GATE), BF16),
                   jax.ShapeDtypeStruct((SSD_GROUPS, rows, GROUP_U), F32),
                   jax.ShapeDtypeStruct((rows, LANES), F32)],
        scratch_shapes=[pltpu.VMEM((TM, D_MODEL), BF16),
                        pltpu.VMEM((CONV_HALO + TM, GROUP_CONV), F32),
                        pltpu.VMEM((SSD_GROUPS, CONV_HALO, GROUP_CONV), F32)],
        compiler_params=_cparams("arbitrary", "arbitrary"),
        name="projections",
    )(*([x] * CPT), meta, gain, w_groups, cw_groups, cb_groups, w_dt)


def _pool_mixer(u_ref, uprev_ref, pw_ref, ps_ref, wo_ref, ubuf, tpos):
    ubuf[:, 0:POOL_HALO, :] = uprev_ref[...]
    ubuf[:, POOL_HALO:POOL_HALO + TM, :] = u_ref[...]
    parts = []
    for k, w in enumerate(POOL_WINDOWS):
        acc = ubuf[k]
        shift = 1
        while shift < w:
            acc = acc + pltpu.roll(acc, shift, axis=0)
            shift *= 2
        cur = ubuf[k, POOL_HALO:POOL_HALO + TM, :]
        cnt = jnp.clip(tpos + 1, 1, w).astype(F32)
        d = acc[POOL_HALO:, :] / cnt - cur
        parts.append(jnp.dot(d.astype(BF16), pw_ref[k], preferred_element_type=F32))
    pm = jnp.concatenate(parts, axis=-1) * ps_ref[...]
    return jnp.dot(pm.astype(BF16), wo_ref[...], preferred_element_type=F32)


def _ssd_chunk(c, k, z_ref, xbc_ref, dt_ref, dtb_ref, alog_ref, dsk_ref, nrm_ref, y_ref, st_ref, yn_ref):
    rowi = lax.broadcasted_iota(I32, (CHUNK, 1), 0)
    coli = lax.broadcasted_iota(I32, (1, CHUNK), 1)
    causal = rowi >= coli
    left = coli < SSD_HEAD_DIM

    dtr = dt_ref[k] + dtb_ref[...]
    dt = jnp.maximum(dtr, 0.0) + jnp.log(1.0 + jnp.exp(-jnp.abs(dtr)))
    dt = jnp.where(rowi >= jnp.where(c == 0, LEAD, 0), dt, 0.0)
    adt2 = dt * (-LOG2E * jnp.exp(alog_ref[...]))
    a2 = jnp.dot(causal.astype(F32), adt2, precision=lax.Precision.HIGHEST,
                 preferred_element_type=F32)
    a2_t = a2.T
    dt_t = dt.T
    dtw_t = dt_t * jnp.exp2(a2_t[:, CHUNK - 1:CHUNK] - a2_t)

    for g in range(SSD_GROUPS):
        b_g = xbc_ref[g, k, :, GROUP_X:GROUP_X + SSD_STATE]
        c_g = xbc_ref[g, k, :, GROUP_X + SSD_STATE:GROUP_CONV]
        cb = lax.dot_general(c_g, b_g, (((1,), (1,)), ((), ())), preferred_element_type=F32)
        b_gt = b_g.T.astype(F32)
        y_off = jnp.dot(c_g, st_ref[g].astype(BF16), preferred_element_type=F32)
        for q in range(PAIRS_PER_GROUP):
            pair = g * PAIRS_PER_GROUP + q
            h0 = 2 * pair
            lanes = slice(pair * LANES, (pair + 1) * LANES)
            qcols = slice(q * LANES, (q + 1) * LANES)
            xq = xbc_ref[g, k, :, qcols]
            a_cols = [jnp.broadcast_to(a2[:, h:h + 1], (CHUNK, CHUNK)) for h in (h0, h0 + 1)]
            a_pair = jnp.where(left, a_cols[0], a_cols[1])
            y_heads, s_heads = [], []
            for j, h in enumerate((h0, h0 + 1)):
                seg = jnp.where(causal, a_cols[j] - a2_t[h:h + 1, :], -jnp.inf)
                m = (cb * jnp.exp2(seg) * dt_t[h:h + 1, :]).astype(BF16)
                y_heads.append(jnp.dot(m, xq, preferred_element_type=F32))
                bw = (b_gt * dtw_t[h:h + 1, :]).astype(BF16)
                s_heads.append(jnp.dot(bw, xq, preferred_element_type=F32))
            y_ref[:, lanes] = (jnp.where(left, y_heads[0], y_heads[1])
                               + y_off[:, qcols] * jnp.exp2(a_pair)
                               + dsk_ref[:, lanes] * xq.astype(F32))
            decay = jnp.exp2(a_pair[CHUNK - 1:CHUNK, :])
            st_ref[g, :, qcols] = st_ref[g, :, qcols] * decay + jnp.where(left, s_heads[0], s_heads[1])

    outs = []
    for g in range(SSD_GROUPS):
        yg = y_ref[:, g * GROUP_WIDTH:(g + 1) * GROUP_WIDTH] * z_ref[g, k].astype(F32)
        outs.append(yg * lax.rsqrt(jnp.mean(yg * yg, axis=-1, keepdims=True) + EPS))
    yn = jnp.concatenate(outs, axis=-1) * nrm_ref[...]
    slot = lax.rem(c, CPT)
    yn_ref[pl.ds(pl.multiple_of(slot * CHUNK, CHUNK), CHUNK), :] = yn.astype(BF16)


def _ssd_kernel(z_ref, xbc_ref, dt_ref, gate_ref, dtb_ref, alog_ref, dsk_ref, nrm_ref, wo_ref,
                o_ref, *scratch):
    c = pl.program_id(1)

    @pl.when(c == 0)
    def _():
        for k in range(SSD_LANES):
            scratch[3 * k + 1][...] = jnp.zeros_like(scratch[3 * k + 1])

    for k in range(SSD_LANES):
        y_ref, st_ref, yn_ref = scratch[3 * k:3 * k + 3]
        _ssd_chunk(c, k, z_ref, xbc_ref, dt_ref, dtb_ref, alog_ref, dsk_ref, nrm_ref, y_ref, st_ref, yn_ref)

    @pl.when(lax.rem(c, CPT) == CPT - 1)
    def _():
        for k in range(SSD_LANES):
            yo = jnp.dot(scratch[3 * k + 2][...], wo_ref[...], preferred_element_type=F32)
            gate = jnp.concatenate([gate_ref[j, k, :, GROUP_GATE // 2:] for j in range(SSD_GROUPS)], axis=-1)
            o_ref[k] = (gate.astype(F32) * yo).astype(BF16)


def _ssd_branch(z_act, xbc_act, dt_raw, gates, dt_bias, a_log, d_skip, ssd_norm, w_ssd_out, bsz, lp):
    nc = lp // CHUNK
    assert bsz % SSD_LANES == 0
    by_seq = lambda a: a.reshape(a.shape[:-2] + (bsz, lp, a.shape[-1]))
    grouped = lambda nrows, n, f: pl.BlockSpec((SSD_GROUPS, SSD_LANES, nrows, n),
                                               lambda b, c: (0, b, f(c), 0))
    chunk = lambda c: c
    tile = lambda c: c // CPT
    const = lambda b, c: (0, 0)
    out = pl.pallas_call(
        _ssd_kernel,
        grid=(bsz // SSD_LANES, nc),
        in_specs=[grouped(CHUNK, GROUP_Z, chunk),
                  grouped(CHUNK, GROUP_CONV, chunk),
                  pl.BlockSpec((SSD_LANES, CHUNK, LANES), lambda b, c: (b, c, 0)),
                  grouped(TM, GROUP_GATE, tile),
                  pl.BlockSpec((1, LANES), const),
                  pl.BlockSpec((1, LANES), const),
                  pl.BlockSpec((1, SSD_INNER), const),
                  pl.BlockSpec((1, SSD_INNER), const),
                  pl.BlockSpec((SSD_INNER, D_MODEL), const)],
        out_specs=pl.BlockSpec((SSD_LANES, TM, D_MODEL), lambda b, c: (b, c // CPT, 0)),
        out_shape=jax.ShapeDtypeStruct((bsz, lp, D_MODEL), BF16),
        scratch_shapes=[pltpu.VMEM((CHUNK, SSD_INNER), F32),
                        pltpu.VMEM((SSD_GROUPS, SSD_STATE, GROUP_WIDTH), F32),
                        pltpu.VMEM((TM, SSD_INNER), BF16)] * SSD_LANES,
        compiler_params=_cparams("parallel", "arbitrary"),
        name="ssd_branch",
    )(by_seq(z_act), by_seq(xbc_act), by_seq(dt_raw), by_seq(gates),
      dt_bias, a_log, d_skip, ssd_norm, w_ssd_out)
    return out.reshape(bsz * lp, D_MODEL)


ROUTE_ROWS = 8
NT_DIMS = (((1,), (1,)), ((), ()))


def _first_row_of_max(vals, rowid, nrows):
    m = jnp.max(vals, axis=0, keepdims=True)
    idx = jnp.min(jnp.where(vals == m, rowid, nrows), axis=0, keepdims=True)
    return m, idx


def _merge_kernel(*refs, lp):
    h_refs, (u_ref, uprev_ref, gate_ref, ys_ref, pw_ref, ps_ref, wpo_ref,
             wo_ref, nf_ref, wrh_ref, wrl_ref, br_ref,
             h1_ref, hn_ref, route_ref, cnt_ref, carry, ubuf) = refs[:CPT + 1], refs[CPT + 1:]
    i = pl.program_id(0)

    @pl.when(i == 0)
    def _():
        carry[...] = jnp.zeros_like(carry)

    tile_row0 = lax.rem(i * TM, lp)
    tpos = tile_row0 + lax.broadcasted_iota(I32, (TM, 1), 0) - LEAD
    y_pool = _pool_mixer(u_ref, uprev_ref, pw_ref, ps_ref, wpo_ref, ubuf, tpos)
    g_pool = jnp.concatenate([gate_ref[k, :, 0:POOL_GROUP] for k in range(len(POOL_WINDOWS))], axis=-1)
    merged = g_pool.astype(F32) * y_pool + ys_ref[...].astype(F32)
    h1 = _row_tile(h_refs, tile_row0 == 0) + jnp.dot(merged.astype(BF16), wo_ref[...],
                                                     preferred_element_type=F32)
    h1_ref[...] = h1
    hn = h1 * lax.rsqrt(jnp.mean(h1 * h1, axis=-1, keepdims=True) + EPS) * nf_ref[...]
    _store_rows(hn_ref, hn)

    hn_hi = hn.astype(BF16)
    hn_lo = (hn - hn_hi.astype(F32)).astype(BF16)
    logits = (lax.dot_general(wrh_ref[...], hn_hi, NT_DIMS, preferred_element_type=F32)
              + lax.dot_general(wrl_ref[...], hn_hi, NT_DIMS, preferred_element_type=F32)
              + lax.dot_general(wrh_ref[...], hn_lo, NT_DIMS, preferred_element_type=F32)) + br_ref[...]
    gl = logits[0:N_EXPERT_GROUPS, :]
    grow = lax.broadcasted_iota(I32, (N_EXPERT_GROUPS, 1), 0)
    gmax, gidx = _first_row_of_max(gl, grow, N_EXPERT_GROUPS)
    g_p = 1.0 / jnp.sum(jnp.exp(gl - gmax), axis=0, keepdims=True)
    erow = lax.broadcasted_iota(I32, (N_EXPERTS, 1), 0)
    el_all = logits[N_EXPERT_GROUPS:N_EXPERT_GROUPS + N_EXPERTS, :]
    lo_row = gidx * EXPERTS_PER_GROUP
    el = jnp.where((erow >= lo_row) & (erow < lo_row + EXPERTS_PER_GROUP), el_all, -jnp.inf)
    m1, e1 = _first_row_of_max(el, erow, N_EXPERTS)
    m2, e2 = _first_row_of_max(jnp.where(erow == e1, -jnp.inf, el), erow, N_EXPERTS)
    t = jnp.exp(m2 - m1)
    w1 = g_p / (1.0 + t)
    w2 = g_p * t / (1.0 + t)

    coli = lax.broadcasted_iota(I32, (1, TM), 1)
    valid = (tile_row0 + coli) >= LEAD
    xrow = lax.broadcasted_iota(I32, (LANES, 1), 0)
    o1 = jnp.where((xrow == e1) & valid, 1.0, 0.0)
    o2 = jnp.where((xrow == e2) & valid, 1.0, 0.0)
    both = o1 + o2
    earlier = (lax.broadcasted_iota(I32, (TM, 1), 0) < coli).astype(BF16)
    before = jnp.dot(both.astype(BF16), earlier, preferred_element_type=F32) + carry[...]
    r1 = jnp.sum(o1 * before, axis=0, keepdims=True)
    r2 = jnp.sum(o2 * before, axis=0, keepdims=True)
    carry[...] = carry[...] + jnp.sum(both, axis=1, keepdims=True)
    cnt_ref[...] = jnp.broadcast_to(carry[...], cnt_ref.shape)

    zero = jnp.zeros_like(w1)
    route_ref[...] = jnp.concatenate(
        [e1.astype(F32), e2.astype(F32), r1, r2, w1, w2, zero, zero], axis=0)


def _merge_router(x, meta, u, gates, ys, pool_w, pool_scale, w_pool_out, w_out, norm_ffn,
                  wr_hi, wr_lo, b_router, lp):
    rows = x.shape[0] * lp
    ngroups = u.shape[0]
    halo_blocks = TM // POOL_HALO
    tile = lambda i: (i, 0)
    const = lambda i: (0, 0)
    fixed = lambda a: pl.BlockSpec(a.shape, lambda i: (0,) * a.ndim, pipeline_mode=pl.Buffered(1))
    return pl.pallas_call(
        functools.partial(_merge_kernel, lp=lp),
        grid=(rows // TM,),
        in_specs=_row_tile_specs(lp // TM, lambda i: i)
        + [pl.BlockSpec((ngroups, TM, POOL_GROUP), lambda i: (0, i, 0)),
           pl.BlockSpec((ngroups, POOL_HALO, POOL_GROUP),
                        lambda i: (0, jnp.maximum(i * halo_blocks - 1, 0), 0)),
           pl.BlockSpec((ngroups, TM, GROUP_GATE), lambda i: (0, i, 0)),
           pl.BlockSpec((TM, D_MODEL), tile),
           fixed(pool_w), fixed(pool_scale), fixed(w_pool_out), fixed(w_out), fixed(norm_ffn),
           fixed(wr_hi), fixed(wr_lo), fixed(b_router)],
        out_specs=[pl.BlockSpec((TM, D_MODEL), tile),
                   pl.BlockSpec((TM * ROW_TILE, LANES), tile),
                   pl.BlockSpec((ROUTE_ROWS, TM), lambda i: (0, i)),
                   pl.BlockSpec((LANES, LANES), const)],
        out_shape=[jax.ShapeDtypeStruct((rows, D_MODEL), F32),
                   jax.ShapeDtypeStruct((rows * ROW_TILE, LANES), F32),
                   jax.ShapeDtypeStruct((ROUTE_ROWS, rows), F32),
                   jax.ShapeDtypeStruct((LANES, LANES), F32)],
        scratch_shapes=[pltpu.VMEM((LANES, 1), F32),
                        pltpu.VMEM((ngroups, POOL_HALO + TM, POOL_GROUP), F32)],
        compiler_params=_cparams("arbitrary"),
        name="merge_router",
    )(*([x] * CPT), meta, u, u, gates, ys, pool_w, pool_scale, w_pool_out, w_out, norm_ffn,
      wr_hi, wr_lo, b_router)


def _tile_rows(row, nrows=1):
    return pl.ds(pl.multiple_of(row * ROW_TILE, ROW_TILE), nrows * ROW_TILE)


def _row_copy(src, src_row, dst, dst_row, sem):
    return pltpu.make_async_copy(src.at[_tile_rows(src_row)], dst.at[_tile_rows(dst_row)], sem)


def _wait_rows(buf, nrows, sem):
    view = buf.at[pl.ds(0, nrows * ROW_TILE)]
    pltpu.make_async_copy(view, view, sem).wait()


def _load_rows(ref, nrows):
    return jnp.concatenate([ref[pl.ds(j, nrows, stride=ROW_TILE), :] for j in range(ROW_TILE)], axis=-1)


def _store_rows(ref, val):
    for j in range(ROW_TILE):
        ref[pl.ds(j, val.shape[0], stride=ROW_TILE), :] = val[:, j * LANES:(j + 1) * LANES]


def _dispatch_kernel(zb_ref, dest_ref, hn_ref, xs_out, zbuf, sem, zsem):
    first = pl.program_id(1) == 0

    @pl.when((pl.program_id(0) == 0) & first)
    def _():
        zbuf[...] = jnp.zeros_like(zbuf)

        def block_copy(j):
            return pltpu.make_async_copy(zbuf, xs_out.at[_tile_rows(jnp.maximum(zb_ref[j], 0) * MOE_BLOCK, MOE_BLOCK)], zsem)

        def start(j, carry):
            @pl.when(zb_ref[j] >= 0)
            def _():
                block_copy(j).start()
            return carry

        def finish(j, carry):
            @pl.when(zb_ref[j] >= 0)
            def _():
                block_copy(j).wait()
            return carry

        lax.fori_loop(0, ZERO_BLOCKS, start, 0)
        lax.fori_loop(0, ZERO_BLOCKS, finish, 0)

    def issue(blk, lo):
        def body(r, carry):
            for k in range(2):
                _row_copy(hn_ref, blk * CHUNK + r, xs_out, dest_ref[blk, k, r], sem).start(priority=k)
            return carry
        lax.fori_loop(lo, CHUNK, body, 0, unroll=DMA_UNROLL)

    @pl.when(first)
    def _():
        issue(0, LEAD)

    @pl.when(jnp.logical_not(first))
    def _():
        issue(0, 0)

    for blk in range(1, CPT):
        issue(blk, 0)

    @pl.when(first)
    def _():
        for _k in range(2):
            _wait_rows(hn_ref, TM - LEAD, sem)

    @pl.when(jnp.logical_not(first))
    def _():
        for _k in range(2):
            _wait_rows(hn_ref, TM, sem)


def _dispatch(zero_blocks, dest, hn, nrows, bsz, lp):
    tiles = lp // TM
    return pl.pallas_call(
        _dispatch_kernel,
        grid_spec=pltpu.PrefetchScalarGridSpec(
            num_scalar_prefetch=1,
            grid=(bsz, tiles),
            in_specs=[pl.BlockSpec((CPT, 2, CHUNK), lambda b, c, zb: (b * tiles + c, 0, 0),
                                   memory_space=pltpu.SMEM),
                      pl.BlockSpec((TM * ROW_TILE, LANES), lambda b, c, zb: (b * tiles + c, 0))],
            out_specs=pl.BlockSpec(memory_space=pl.ANY),
            scratch_shapes=[pltpu.VMEM((MOE_BLOCK * ROW_TILE, LANES), F32),
                            pltpu.SemaphoreType.DMA(()),
                            pltpu.SemaphoreType.DMA(())]),
        out_shape=jax.ShapeDtypeStruct((nrows * ROW_TILE, LANES), F32),
        compiler_params=_cparams("arbitrary", "arbitrary"),
        name="moe_dispatch",
    )(zero_blocks, dest, hn)


def _expert_kernel(blk0_ref, nblk_ref, nused_ref, x_hbm, wgu_ref, wdn_ref, y_hbm,
                   xbuf, ybuf, wgu16, wdn16, xsem, ysem):
    e = pl.program_id(0)
    nblk = nblk_ref[e]
    nused = nused_ref[0]
    nb = y_hbm.shape[0] // (MOE_BLOCK * ROW_TILE)

    def x_copy(g, slot):
        return pltpu.make_async_copy(x_hbm.at[_tile_rows(g * MOE_BLOCK, MOE_BLOCK)], xbuf.at[slot], xsem.at[slot])

    def y_copy(g, slot):
        return pltpu.make_async_copy(ybuf.at[slot], y_hbm.at[_tile_rows(g * MOE_BLOCK, MOE_BLOCK)], ysem.at[slot])

    @pl.when((e == 0) & (nused > 0))
    def _():
        x_copy(0, 0).start()

    @pl.when(nblk > 0)
    def _():
        wgu16[...] = wgu_ref[0].astype(BF16)
        wdn16[...] = wdn_ref[0].astype(BF16)

        def body(j, carry):
            g = blk0_ref[e] + j
            slot = lax.rem(g, 2)

            @pl.when(g + 1 < nused)
            def _():
                x_copy(g + 1, 1 - slot).start()

            x_copy(g, slot).wait()

            @pl.when(g >= 2)
            def _():
                y_copy(g - 2, slot).wait()

            gu = jnp.dot(_load_rows(xbuf.at[slot], MOE_BLOCK).astype(BF16), wgu16[...],
                         preferred_element_type=F32)
            gate, up = gu[:, :D_EXPERT], gu[:, D_EXPERT:]
            act = gate * _sigmoid(gate) * up
            _store_rows(ybuf.at[slot], jnp.dot(act.astype(BF16), wdn16[...], preferred_element_type=F32))
            y_copy(g, slot).start()
            return carry

        lax.fori_loop(0, nblk, body, 0)

    @pl.when(e == pl.num_programs(0) - 1)
    def _():
        @pl.when(nused >= 2)
        def _():
            y_copy(nused - 2, lax.rem(nused, 2)).wait()

        @pl.when(nused >= 1)
        def _():
            y_copy(nused - 1, lax.rem(nused + 1, 2)).wait()

        ybuf[0] = jnp.zeros((MOE_BLOCK * ROW_TILE, LANES), F32)

        def zero_start(g, carry):
            y_copy(g, 0).start()
            return carry

        def zero_wait(g, carry):
            y_copy(g, 0).wait()
            return carry

        lax.fori_loop(nused, nb, zero_start, 0)
        lax.fori_loop(nused, nb, zero_wait, 0)


def _experts(blk0, nblk, nused, xs, w_gu, w_dn):
    wsel = lambda e, b0, nbk, nu: (e, 0, 0)
    return pl.pallas_call(
        _expert_kernel,
        grid_spec=pltpu.PrefetchScalarGridSpec(
            num_scalar_prefetch=3,
            grid=(N_EXPERTS,),
            in_specs=[pl.BlockSpec(memory_space=pl.ANY),
                      pl.BlockSpec((1, D_MODEL, 2 * D_EXPERT), wsel),
                      pl.BlockSpec((1, D_EXPERT, D_MODEL), wsel)],
            out_specs=pl.BlockSpec(memory_space=pl.ANY),
            scratch_shapes=[pltpu.VMEM((2, MOE_BLOCK * ROW_TILE, LANES), F32),
                            pltpu.VMEM((2, MOE_BLOCK * ROW_TILE, LANES), F32),
                            pltpu.VMEM((D_MODEL, 2 * D_EXPERT), BF16),
                            pltpu.VMEM((D_EXPERT, D_MODEL), BF16),
                            pltpu.SemaphoreType.DMA((2,)),
                            pltpu.SemaphoreType.DMA((2,))]),
        out_shape=jax.ShapeDtypeStruct(xs.shape, F32),
        compiler_params=_cparams("arbitrary"),
        name="moe_experts",
    )(blk0, nblk, nused, xs, w_gu, w_dn)


def _combine_kernel(da_ref, db_ref, dn_ref, h1a_ref, h1b_ref, rta_ref, rtb_ref, nf_ref, y_hbm,
                    o_ref, ya, yb, sem):
    step = pl.program_id(0) * pl.num_programs(1) + pl.program_id(1)
    nsteps = pl.num_programs(0) * pl.num_programs(1)

    def issue(d_ref, slot):
        def body(r, carry):
            _row_copy(y_hbm, d_ref[0, 0, r], ya.at[slot], r, sem.at[slot, 0]).start(priority=0)
            _row_copy(y_hbm, d_ref[0, 1, r], yb.at[slot], r, sem.at[slot, 1]).start(priority=1)
            return carry
        lax.fori_loop(0, CHUNK, body, 0, unroll=DMA_UNROLL)

    def finish(slot, h1_ref, rt_ref):
        _wait_rows(ya.at[slot], CHUNK, sem.at[slot, 0])
        _wait_rows(yb.at[slot], CHUNK, sem.at[slot, 1])
        rt = rt_ref[...]
        h2 = (h1_ref[...] + rt[:, 0:1] * _load_rows(ya.at[slot], CHUNK)
              + rt[:, 1:2] * _load_rows(yb.at[slot], CHUNK))
        o_ref[0, slot * CHUNK:(slot + 1) * CHUNK, :] = (
            h2 * lax.rsqrt(jnp.mean(h2 * h2, axis=-1, keepdims=True) + EPS) * nf_ref[...])

    @pl.when(step == 0)
    def _():
        issue(da_ref, 0)

    issue(db_ref, 1)
    finish(0, h1a_ref, rta_ref)

    @pl.when(step + 1 < nsteps)
    def _():
        issue(dn_ref, 0)

    finish(1, h1b_ref, rtb_ref)


def _combine(dest, h1, wts, norm_final, ys, bsz, lp, seq):
    nc = lp // CHUNK
    half = seq // (2 * CHUNK)
    first = lambda b, j: b * nc + 1 + 2 * j
    second = lambda b, j: b * nc + 2 + 2 * j

    def upcoming(b, j):
        nxt = jnp.minimum(b * half + j + 1, bsz * half - 1)
        return first(nxt // half, nxt % half)

    smem3 = lambda f: pl.BlockSpec((1, 2, CHUNK), lambda b, j: (f(b, j), 0, 0), memory_space=pltpu.SMEM)
    rows2 = lambda f, width: pl.BlockSpec((CHUNK, width), lambda b, j: (f(b, j), 0))
    return pl.pallas_call(
        _combine_kernel,
        grid=(bsz, half),
        in_specs=[smem3(first), smem3(second), smem3(upcoming),
                  rows2(first, D_MODEL), rows2(second, D_MODEL),
                  rows2(first, 2), rows2(second, 2),
                  pl.BlockSpec((1, D_MODEL), lambda b, j: (0, 0)),
                  pl.BlockSpec(memory_space=pl.ANY)],
        out_specs=pl.BlockSpec((1, 2 * CHUNK, D_MODEL), lambda b, j: (b, j, 0)),
        out_shape=jax.ShapeDtypeStruct((bsz, seq, D_MODEL), F32),
        scratch_shapes=[pltpu.VMEM((2, CHUNK * ROW_TILE, LANES), F32),
                        pltpu.VMEM((2, CHUNK * ROW_TILE, LANES), F32),
                        pltpu.SemaphoreType.DMA((2, 2))],
        compiler_params=_cparams("arbitrary", "arbitrary"),
        name="moe_combine",
    )(dest, dest, dest, h1, h1, wts, wts, norm_final, ys)


def _row(v):
    return v.reshape(1, -1).astype(F32)


def _pad_lanes(v):
    return jnp.pad(v, ((0, 0), (0, LANES - v.shape[1])))


def kernel(x, meta_tokens, norm_mix, w_in, pool_w, pool_scale, conv_w, conv_b, dt_bias, a_log, d_skip,
           ssd_norm, w_pool_out, w_ssd_out, w_out, norm_ffn, w_router_group, b_router_group,
           w_router_expert, b_router_expert, w_gate_up, w_down, norm_final):
    bsz, seq, _ = x.shape
    lp = LEAD + N_META + seq
    assert lp % TM == 0 and seq % (2 * CHUNK) == 0
    rows = bsz * lp

    meta = meta_tokens.astype(x.dtype)

    wi = w_in[0]
    o_z, o_x, o_dt, o_gp, o_gs = 1024, 3072, 6144, 6176, 7200
    cw = conv_w[0].astype(F32)
    cb = _row(conv_b[0])
    grp = lambda a, off, g, n: a[:, off + g * n:off + (g + 1) * n]
    w_groups, cw_groups, cb_groups = [], [], []
    for g in range(SSD_GROUPS):
        conv_cols = lambda a: jnp.concatenate(
            [grp(a, 0, g, GROUP_X), grp(a, SSD_INNER, g, SSD_STATE),
             grp(a, SSD_INNER + SSD_GROUPS * SSD_STATE, g, SSD_STATE)], axis=1)
        half_cols = jnp.concatenate([grp(wi, o_z, g, GROUP_Z), grp(wi, o_gp, g, GROUP_GATE // 2),
                                     grp(wi, o_gs, g, GROUP_GATE // 2)], axis=1)
        w_groups.append(jnp.concatenate([conv_cols(wi[:, o_x:o_dt]), 0.5 * half_cols,
                                         grp(wi, 0, g, GROUP_U)], axis=1).astype(BF16))
        cw_groups.append(0.5 * conv_cols(cw))
        cb_groups.append(0.5 * conv_cols(cb))
    xbc_act, z_act, gates, u, dt_raw = _projections(
        x, meta, _row(norm_mix[0]), jnp.stack(w_groups), jnp.stack(cw_groups), jnp.stack(cb_groups),
        _pad_lanes(wi[:, o_dt:o_gp]).astype(BF16), lp)

    ys = _ssd_branch(z_act, xbc_act, dt_raw, gates,
                     _pad_lanes(_row(dt_bias[0])), _pad_lanes(_row(a_log[0])),
                     _row(jnp.repeat(d_skip[0], SSD_HEAD_DIM)), _row(ssd_norm[0]),
                     w_ssd_out[0].astype(BF16), bsz, lp)

    w_router = jnp.concatenate([w_router_group[0], w_router_expert[0]], axis=1).astype(F32).T
    w_router = jnp.pad(w_router, ((0, LANES - w_router.shape[0]), (0, 0)))
    wr_hi = w_router.astype(BF16)
    wr_lo = (w_router - wr_hi.astype(F32)).astype(BF16)
    b_router = _pad_lanes(jnp.concatenate([_row(b_router_group[0]), _row(b_router_expert[0])], axis=1))
    b_router = jnp.broadcast_to(b_router.reshape(LANES, 1), (LANES, TM))
    h1, hn, route, counts = _merge_router(x, meta, u, gates, ys, pool_w[0].astype(BF16), _row(pool_scale[0]),
                                          w_pool_out[0].astype(BF16), w_out[0].astype(BF16),
                                          _row(norm_ffn[0]), wr_hi, wr_lo, b_router, lp)

    n_assign = 2 * bsz * (N_META + seq)
    nb = -(-n_assign // MOE_BLOCK) + N_EXPERTS
    cnt = counts[:N_EXPERTS, 0].astype(I32)
    pcnt = (cnt + MOE_BLOCK - 1) // MOE_BLOCK * MOE_BLOCK
    pends = jnp.cumsum(pcnt)
    pstarts = pends - pcnt
    experts = route[0:2].astype(I32)
    eids = jnp.arange(N_EXPERTS, dtype=I32)[:, None]
    start_of = jnp.sum(jnp.where(experts[:, None, :] == eids, pstarts[:, None], 0), axis=1)
    dest = start_of + route[2:4].astype(I32)
    dest = dest.reshape(2, rows // CHUNK, CHUNK).transpose(1, 0, 2)
    wts = route[4:6].T
    nused = (pends[-1:] // MOE_BLOCK).astype(I32)
    last_blocks = jnp.where(pcnt > 0, pends // MOE_BLOCK - 1, -1)
    tail_ids = nused[0] + jnp.arange(ZERO_BLOCKS - N_EXPERTS, dtype=I32)
    zero_blocks = jnp.concatenate([last_blocks, jnp.where(tail_ids < nb, tail_ids, -1)]).astype(I32)

    xs = _dispatch(zero_blocks, dest, hn, nb * MOE_BLOCK, bsz, lp)
    yexp = _experts((pstarts // MOE_BLOCK).astype(I32), (pcnt // MOE_BLOCK).astype(I32), nused,
                    xs, w_gate_up[0], w_down[0])
    return _combine(dest, h1, wts, _row(norm_final), yexp, bsz, lp, seq)
```

```python
import functools
import math

import jax
import jax.numpy as jnp
from jax import lax
from jax.experimental import pallas as pl
from jax.experimental.pallas import tpu as pltpu

F32 = jnp.float32
BF16 = jnp.bfloat16
I32 = jnp.int32

D_MODEL = 1024
N_META = 16
EPS = 1e-6
LOG2E = math.log2(math.e)
CHUNK = 128
LEAD = CHUNK - N_META
CPT = 5
TM = CPT * CHUNK
POOL_WINDOWS = (2, 4, 8, 16)
POOL_GROUP = D_MODEL // len(POOL_WINDOWS)
POOL_HALO = 16
SSD_INNER = 2 * D_MODEL
SSD_HEADS = 32
SSD_HEAD_DIM = 64
SSD_GROUPS = 4
SSD_STATE = 128
SSD_CONV = 4
CONV_HALO = 8
BC_WIDTH = 2 * SSD_GROUPS * SSD_STATE
GROUP_WIDTH = SSD_INNER // SSD_GROUPS
PAIRS_PER_GROUP = GROUP_WIDTH // 128
SSD_LANES = 2
N_EXPERT_GROUPS = 8
EXPERTS_PER_GROUP = 8
N_EXPERTS = 64
D_EXPERT = 512
MOE_BLOCK = 512
LANES = 128
DMA_UNROLL = True
U32 = jnp.uint32
ROW_TILE = D_MODEL // LANES // 2
ZERO_BLOCKS = 2 * N_EXPERTS
STRIP = 256
GROUP_X = SSD_INNER // SSD_GROUPS
GROUP_CONV = GROUP_X + 2 * SSD_STATE
GROUP_Z = SSD_INNER // SSD_GROUPS
GROUP_GATE = 2 * D_MODEL // SSD_GROUPS
GROUP_U = D_MODEL // SSD_GROUPS
GROUP_COLS = GROUP_CONV + GROUP_Z + GROUP_GATE + GROUP_U

VMEM_LIMIT = 48 * 1024 * 1024


def _cparams(*sem):
    return pltpu.CompilerParams(dimension_semantics=sem, vmem_limit_bytes=VMEM_LIMIT)


def _sigmoid(x):
    return 1.0 / (1.0 + jnp.exp(-x))


def _silu_of_half(h):
    return h + h * jnp.tanh(h)


def _normed(x, g_ref):
    return (x * lax.rsqrt(jnp.mean(x * x, axis=-1, keepdims=True) + EPS) * g_ref[...]).astype(BF16)


def _row_tile_specs(tiles_per_seq, tile_of):
    def chunk_spec(j):
        def index(*ids):
            tile = tile_of(*ids)
            return tile // tiles_per_seq, jnp.maximum(lax.rem(tile, tiles_per_seq) * CPT - 1 + j, 0), 0
        return pl.BlockSpec((1, CHUNK, D_MODEL), index)
    return [chunk_spec(j) for j in range(CPT)] + [pl.BlockSpec((N_META, D_MODEL), lambda *ids: (0, 0))]


def _row_tile(refs, first_tile_of_seq):
    chunk_refs, meta_ref = refs[:CPT], refs[CPT]
    lead = jnp.concatenate([jnp.zeros((LEAD, D_MODEL), F32), meta_ref[...]], axis=0)
    first = jnp.where(first_tile_of_seq, lead, chunk_refs[0][0])
    return jnp.concatenate([first] + [r[0] for r in chunk_refs[1:]], axis=0)


def _proj_kernel(*refs, lp):
    h_refs, (g_ref, w_ref, cw_ref, cb_ref, wdt_ref,
             xbc_ref, z_ref, gate_ref, u_ref, dt_ref, hn_ref, pbuf, tails) = refs[:CPT + 1], refs[CPT + 1:]
    i, grp = pl.program_id(0), pl.program_id(1)

    @pl.when(grp == 0)
    def _():
        hn_ref[...] = _normed(_row_tile(h_refs, lax.rem(i * TM, lp) == 0), g_ref)
        dt_ref[...] = jnp.dot(hn_ref[...], wdt_ref[...], preferred_element_type=F32)

    @pl.when(i == 0)
    def _():
        tails[grp] = jnp.zeros((CONV_HALO, GROUP_CONV), F32)

    pbuf[0:CONV_HALO, :] = tails[grp]
    hn = hn_ref[...]
    proj = lambda c0, n: jnp.dot(hn, w_ref[0, :, c0:c0 + n], preferred_element_type=F32)

    row = lax.rem(i * TM, lp) + lax.broadcasted_iota(I32, (CHUNK, 1), 0)
    lead_mask = row >= LEAD
    for c0 in range(0, GROUP_CONV, STRIP):
        cols = slice(c0, c0 + STRIP)
        pbuf[CONV_HALO:, cols] = proj(c0, STRIP)
        for r0 in range(0, TM, CHUNK):
            xa = pbuf[r0:r0 + CONV_HALO + CHUNK, cols]
            half = cb_ref[0, :, cols] + cw_ref[0, SSD_CONV - 1:SSD_CONV, cols] * xa[CONV_HALO:, :]
            for s in range(1, SSD_CONV):
                tap = cw_ref[0, SSD_CONV - 1 - s:SSD_CONV - s, cols]
                half = half + tap * pltpu.roll(xa, s, axis=0)[CONV_HALO:, :]
            act = _silu_of_half(half)
            if r0 == 0:
                act = jnp.where(lead_mask, act, 0.0)
            xbc_ref[0, r0:r0 + CHUNK, cols] = act.astype(BF16)
    tails[grp] = pbuf[TM:TM + CONV_HALO, :]

    for c0 in range(0, GROUP_Z, STRIP):
        z_ref[0, :, c0:c0 + STRIP] = _silu_of_half(proj(GROUP_CONV + c0, STRIP)).astype(BF16)
    for c0 in range(0, GROUP_GATE, STRIP):
        half = proj(GROUP_CONV + GROUP_Z + c0, STRIP)
        gate_ref[0, :, c0:c0 + STRIP] = (0.5 + 0.5 * jnp.tanh(half)).astype(BF16)
    u_ref[0] = proj(GROUP_CONV + GROUP_Z + GROUP_GATE, GROUP_U)


def _projections(x, meta, gain, w_groups, cw_groups, cb_groups, w_dt, lp):
    rows = x.shape[0] * lp
    tiles = rows // TM
    per_group = lambda n: pl.BlockSpec((1, TM, n), lambda i, g: (g, i, 0))
    weights = lambda a: pl.BlockSpec((1,) + a.shape[1:], lambda i, g: (g, 0, 0))
    return pl.pallas_call(
        functools.partial(_proj_kernel, lp=lp),
        grid=(tiles, SSD_GROUPS),
        in_specs=_row_tile_specs(lp // TM, lambda i, g: i)
        + [pl.BlockSpec((1, D_MODEL), lambda i, g: (0, 0)),
           weights(w_groups), weights(cw_groups), weights(cb_groups),
           pl.BlockSpec(w_dt.shape, lambda i, g: (0, 0))],
        out_specs=[per_group(GROUP_CONV), per_group(GROUP_Z), per_group(GROUP_GATE), per_group(GROUP_U),
                   pl.BlockSpec((TM, LANES), lambda i, g: (i, 0))],
        out_shape=[jax.ShapeDtypeStruct((SSD_GROUPS, rows, GROUP_CONV), BF16),
                   jax.ShapeDtypeStruct((SSD_GROUPS, rows, GROUP_Z), BF16),
                   jax.ShapeDtypeStruct((SSD_GROUPS, rows, GROUP_GATE), BF16),
                   jax.ShapeDtypeStruct((SSD_GROUPS, rows, GROUP_U), F32),
                   jax.ShapeDtypeStruct((rows, LANES), F32)],
        scratch_shapes=[pltpu.VMEM((TM, D_MODEL), BF16),
                        pltpu.VMEM((CONV_HALO + TM, GROUP_CONV), F32),
                        pltpu.VMEM((SSD_GROUPS, CONV_HALO, GROUP_CONV), F32)],
        compiler_params=_cparams("arbitrary", "arbitrary"),
        name="projections",
    )(*([x] * CPT), meta, gain, w_groups, cw_groups, cb_groups, w_dt)


def _pool_mixer(u_ref, uprev_ref, pw_ref, ps_ref, wo_ref, ubuf, tpos):
    ubuf[:, 0:POOL_HALO, :] = uprev_ref[...]
    ubuf[:, POOL_HALO:POOL_HALO + TM, :] = u_ref[...]
    parts = []
    for k, w in enumerate(POOL_WINDOWS):
        acc = ubuf[k]
        shift = 1
        while shift < w:
            acc = acc + pltpu.roll(acc, shift, axis=0)
            shift *= 2
        cur = ubuf[k, POOL_HALO:POOL_HALO + TM, :]
        cnt = jnp.clip(tpos + 1, 1, w).astype(F32)
        d = acc[POOL_HALO:, :] / cnt - cur
        parts.append(jnp.dot(d.astype(BF16), pw_ref[k], preferred_element_type=F32))
    pm = jnp.concatenate(parts, axis=-1) * ps_ref[...]
    return jnp.dot(pm.astype(BF16), wo_ref[...], preferred_element_type=F32)


def _ssd_chunk(c, k, z_ref, xbc_ref, dt_ref, dtb_ref, alog_ref, dsk_ref, nrm_ref, y_ref, st_ref, yn_ref):
    rowi = lax.broadcasted_iota(I32, (CHUNK, 1), 0)
    coli = lax.broadcasted_iota(I32, (1, CHUNK), 1)
    causal = rowi >= coli
    left = coli < SSD_HEAD_DIM

    dtr = dt_ref[k] + dtb_ref[...]
    dt = jnp.maximum(dtr, 0.0) + jnp.log(1.0 + jnp.exp(-jnp.abs(dtr)))
    dt = jnp.where(rowi >= jnp.where(c == 0, LEAD, 0), dt, 0.0)
    adt2 = dt * (-LOG2E * jnp.exp(alog_ref[...]))
    a2 = jnp.dot(causal.astype(F32), adt2, precision=lax.Precision.HIGHEST,
                 preferred_element_type=F32)
    a2_t = a2.T
    dt_t = dt.T
    dtw_t = dt_t * jnp.exp2(a2_t[:, CHUNK - 1:CHUNK] - a2_t)

    for g in range(SSD_GROUPS):
        b_g = xbc_ref[g, k, :, GROUP_X:GROUP_X + SSD_STATE]
        c_g = xbc_ref[g, k, :, GROUP_X + SSD_STATE:GROUP_CONV]
        cb = lax.dot_general(c_g, b_g, (((1,), (1,)), ((), ())), preferred_element_type=F32)
        b_gt = b_g.T.astype(F32)
        y_off = jnp.dot(c_g, st_ref[g].astype(BF16), preferred_element_type=F32)
        for q in range(PAIRS_PER_GROUP):
            pair = g * PAIRS_PER_GROUP + q
            h0 = 2 * pair
            lanes = slice(pair * LANES, (pair + 1) * LANES)
            qcols = slice(q * LANES, (q + 1) * LANES)
            xq = xbc_ref[g, k, :, qcols]
            x2 = jnp.concatenate([jnp.where(left, xq, jnp.zeros_like(xq)),
                                  jnp.where(left, jnp.zeros_like(xq), xq)], axis=0)
            a_cols = [jnp.broadcast_to(a2[:, h:h + 1], (CHUNK, CHUNK)) for h in (h0, h0 + 1)]
            a_pair = jnp.where(left, a_cols[0], a_cols[1])
            scores, b_scaled = [], []
            for j, h in enumerate((h0, h0 + 1)):
                seg = jnp.where(causal, a_cols[j] - a2_t[h:h + 1, :], -jnp.inf)
                scores.append((cb * jnp.exp2(seg) * dt_t[h:h + 1, :]).astype(BF16))
                b_scaled.append((b_gt * dtw_t[h:h + 1, :]).astype(BF16))
            lhs = jnp.concatenate([jnp.concatenate(scores, axis=1), jnp.concatenate(b_scaled, axis=1)], axis=0)
            both = jnp.dot(lhs, x2, preferred_element_type=F32)
            y_ref[:, lanes] = (both[:CHUNK, :] + y_off[:, qcols] * jnp.exp2(a_pair)
                               + dsk_ref[:, lanes] * xq.astype(F32))
            decay = jnp.exp2(a_pair[CHUNK - 1:CHUNK, :])
            st_ref[g, :, qcols] = st_ref[g, :, qcols] * decay + both[CHUNK:, :]

    outs = []
    for g in range(SSD_GROUPS):
        yg = y_ref[:, g * GROUP_WIDTH:(g + 1) * GROUP_WIDTH] * z_ref[g, k].astype(F32)
        outs.append(yg * lax.rsqrt(jnp.mean(yg * yg, axis=-1, keepdims=True) + EPS))
    yn = jnp.concatenate(outs, axis=-1) * nrm_ref[...]
    slot = lax.rem(c, CPT)
    yn_ref[pl.ds(pl.multiple_of(slot * CHUNK, CHUNK), CHUNK), :] = yn.astype(BF16)


def _ssd_kernel(z_ref, xbc_ref, dt_ref, gate_ref, dtb_ref, alog_ref, dsk_ref, nrm_ref, wo_ref,
                o_ref, *scratch):
    c = pl.program_id(1)

    @pl.when(c == 0)
    def _():
        for k in range(SSD_LANES):
            scratch[3 * k + 1][...] = jnp.zeros_like(scratch[3 * k + 1])

    for k in range(SSD_LANES):
        y_ref, st_ref, yn_ref = scratch[3 * k:3 * k + 3]
        _ssd_chunk(c, k, z_ref, xbc_ref, dt_ref, dtb_ref, alog_ref, dsk_ref, nrm_ref, y_ref, st_ref, yn_ref)

    @pl.when(lax.rem(c, CPT) == CPT - 1)
    def _():
        for k in range(SSD_LANES):
            yo = jnp.dot(scratch[3 * k + 2][...], wo_ref[...], preferred_element_type=F32)
            gate = jnp.concatenate([gate_ref[j, k, :, GROUP_GATE // 2:] for j in range(SSD_GROUPS)], axis=-1)
            o_ref[k] = (gate.astype(F32) * yo).astype(BF16)


def _ssd_branch(z_act, xbc_act, dt_raw, gates, dt_bias, a_log, d_skip, ssd_norm, w_ssd_out, bsz, lp):
    nc = lp // CHUNK
    assert bsz % SSD_LANES == 0
    by_seq = lambda a: a.reshape(a.shape[:-2] + (bsz, lp, a.shape[-1]))
    grouped = lambda nrows, n, f: pl.BlockSpec((SSD_GROUPS, SSD_LANES, nrows, n),
                                               lambda b, c: (0, b, f(c), 0))
    chunk = lambda c: c
    tile = lambda c: c // CPT
    const = lambda b, c: (0, 0)
    out = pl.pallas_call(
        _ssd_kernel,
        grid=(bsz // SSD_LANES, nc),
        in_specs=[grouped(CHUNK, GROUP_Z, chunk),
                  grouped(CHUNK, GROUP_CONV, chunk),
                  pl.BlockSpec((SSD_LANES, CHUNK, LANES), lambda b, c: (b, c, 0)),
                  grouped(TM, GROUP_GATE, tile),
                  pl.BlockSpec((1, LANES), const),
                  pl.BlockSpec((1, LANES), const),
                  pl.BlockSpec((1, SSD_INNER), const),
                  pl.BlockSpec((1, SSD_INNER), const),
                  pl.BlockSpec((SSD_INNER, D_MODEL), const)],
        out_specs=pl.BlockSpec((SSD_LANES, TM, D_MODEL), lambda b, c: (b, c // CPT, 0)),
        out_shape=jax.ShapeDtypeStruct((bsz, lp, D_MODEL), BF16),
        scratch_shapes=[pltpu.VMEM((CHUNK, SSD_INNER), F32),
                        pltpu.VMEM((SSD_GROUPS, SSD_STATE, GROUP_WIDTH), F32),
                        pltpu.VMEM((TM, SSD_INNER), BF16)] * SSD_LANES,
        compiler_params=_cparams("parallel", "arbitrary"),
        name="ssd_branch",
    )(by_seq(z_act), by_seq(xbc_act), by_seq(dt_raw), by_seq(gates),
      dt_bias, a_log, d_skip, ssd_norm, w_ssd_out)
    return out.reshape(bsz * lp, D_MODEL)


ROUTE_ROWS = 8
NT_DIMS = (((1,), (1,)), ((), ()))


def _first_row_of_max(vals, rowid, nrows):
    m = jnp.max(vals, axis=0, keepdims=True)
    idx = jnp.min(jnp.where(vals == m, rowid, nrows), axis=0, keepdims=True)
    return m, idx


def _merge_kernel(*refs, lp):
    h_refs, (u_ref, uprev_ref, gate_ref, ys_ref, pw_ref, ps_ref, wpo_ref,
             wo_ref, nf_ref, wrh_ref, wrl_ref, br_ref,
             h1_ref, hn_ref, route_ref, cnt_ref, carry, ubuf) = refs[:CPT + 1], refs[CPT + 1:]
    i = pl.program_id(0)

    @pl.when(i == 0)
    def _():
        carry[...] = jnp.zeros_like(carry)

    tile_row0 = lax.rem(i * TM, lp)
    tpos = tile_row0 + lax.broadcasted_iota(I32, (TM, 1), 0) - LEAD
    y_pool = _pool_mixer(u_ref, uprev_ref, pw_ref, ps_ref, wpo_ref, ubuf, tpos)
    g_pool = jnp.concatenate([gate_ref[k, :, 0:POOL_GROUP] for k in range(len(POOL_WINDOWS))], axis=-1)
    merged = g_pool.astype(F32) * y_pool + ys_ref[...].astype(F32)
    h1 = _row_tile(h_refs, tile_row0 == 0) + jnp.dot(merged.astype(BF16), wo_ref[...],
                                                     preferred_element_type=F32)
    h1_ref[...] = h1
    hn = h1 * lax.rsqrt(jnp.mean(h1 * h1, axis=-1, keepdims=True) + EPS) * nf_ref[...]
    _store_rows(hn_ref, hn)

    hn_hi = hn.astype(BF16)
    hn_lo = (hn - hn_hi.astype(F32)).astype(BF16)
    logits = (lax.dot_general(wrh_ref[...], hn_hi, NT_DIMS, preferred_element_type=F32)
              + lax.dot_general(wrl_ref[...], hn_hi, NT_DIMS, preferred_element_type=F32)
              + lax.dot_general(wrh_ref[...], hn_lo, NT_DIMS, preferred_element_type=F32)) + br_ref[...]
    gl = logits[0:N_EXPERT_GROUPS, :]
    grow = lax.broadcasted_iota(I32, (N_EXPERT_GROUPS, 1), 0)
    gmax, gidx = _first_row_of_max(gl, grow, N_EXPERT_GROUPS)
    g_p = 1.0 / jnp.sum(jnp.exp(gl - gmax), axis=0, keepdims=True)
    erow = lax.broadcasted_iota(I32, (N_EXPERTS, 1), 0)
    el_all = logits[N_EXPERT_GROUPS:N_EXPERT_GROUPS + N_EXPERTS, :]
    lo_row = gidx * EXPERTS_PER_GROUP
    el = jnp.where((erow >= lo_row) & (erow < lo_row + EXPERTS_PER_GROUP), el_all, -jnp.inf)
    m1, e1 = _first_row_of_max(el, erow, N_EXPERTS)
    m2, e2 = _first_row_of_max(jnp.where(erow == e1, -jnp.inf, el), erow, N_EXPERTS)
    t = jnp.exp(m2 - m1)
    w1 = g_p / (1.0 + t)
    w2 = g_p * t / (1.0 + t)

    coli = lax.broadcasted_iota(I32, (1, TM), 1)
    valid = (tile_row0 + coli) >= LEAD
    xrow = lax.broadcasted_iota(I32, (LANES, 1), 0)
    o1 = jnp.where((xrow == e1) & valid, 1.0, 0.0)
    o2 = jnp.where((xrow == e2) & valid, 1.0, 0.0)
    both = o1 + o2
    earlier = (lax.broadcasted_iota(I32, (TM, 1), 0) < coli).astype(BF16)
    before = jnp.dot(both.astype(BF16), earlier, preferred_element_type=F32) + carry[...]
    r1 = jnp.sum(o1 * before, axis=0, keepdims=True)
    r2 = jnp.sum(o2 * before, axis=0, keepdims=True)
    carry[...] = carry[...] + jnp.sum(both, axis=1, keepdims=True)
    cnt_ref[...] = jnp.broadcast_to(carry[...], cnt_ref.shape)

    zero = jnp.zeros_like(w1)
    route_ref[...] = jnp.concatenate(
        [e1.astype(F32), e2.astype(F32), r1, r2, w1, w2, zero, zero], axis=0)


def _merge_router(x, meta, u, gates, ys, pool_w, pool_scale, w_pool_out, w_out, norm_ffn,
                  wr_hi, wr_lo, b_router, lp):
    rows = x.shape[0] * lp
    ngroups = u.shape[0]
    halo_blocks = TM // POOL_HALO
    tile = lambda i: (i, 0)
    const = lambda i: (0, 0)
    fixed = lambda a: pl.BlockSpec(a.shape, lambda i: (0,) * a.ndim, pipeline_mode=pl.Buffered(1))
    return pl.pallas_call(
        functools.partial(_merge_kernel, lp=lp),
        grid=(rows // TM,),
        in_specs=_row_tile_specs(lp // TM, lambda i: i)
        + [pl.BlockSpec((ngroups, TM, POOL_GROUP), lambda i: (0, i, 0)),
           pl.BlockSpec((ngroups, POOL_HALO, POOL_GROUP),
                        lambda i: (0, jnp.maximum(i * halo_blocks - 1, 0), 0)),
           pl.BlockSpec((ngroups, TM, GROUP_GATE), lambda i: (0, i, 0)),
           pl.BlockSpec((TM, D_MODEL), tile),
           fixed(pool_w), fixed(pool_scale), fixed(w_pool_out), fixed(w_out), fixed(norm_ffn),
           fixed(wr_hi), fixed(wr_lo), fixed(b_router)],
        out_specs=[pl.BlockSpec((TM, D_MODEL), tile),
                   pl.BlockSpec((TM * ROW_TILE, LANES), tile),
                   pl.BlockSpec((ROUTE_ROWS, TM), lambda i: (0, i)),
                   pl.BlockSpec((LANES, LANES), const)],
        out_shape=[jax.ShapeDtypeStruct((rows, D_MODEL), F32),
                   jax.ShapeDtypeStruct((rows * ROW_TILE, LANES), U32),
                   jax.ShapeDtypeStruct((ROUTE_ROWS, rows), F32),
                   jax.ShapeDtypeStruct((LANES, LANES), F32)],
        scratch_shapes=[pltpu.VMEM((LANES, 1), F32),
                        pltpu.VMEM((ngroups, POOL_HALO + TM, POOL_GROUP), F32)],
        compiler_params=_cparams("arbitrary"),
        name="merge_router",
    )(*([x] * CPT), meta, u, u, gates, ys, pool_w, pool_scale, w_pool_out, w_out, norm_ffn,
      wr_hi, wr_lo, b_router)


def _tile_rows(row, nrows=1):
    return pl.ds(pl.multiple_of(row * ROW_TILE, ROW_TILE), nrows * ROW_TILE)


def _row_copy(src, src_row, dst, dst_row, sem):
    return pltpu.make_async_copy(src.at[_tile_rows(src_row)], dst.at[_tile_rows(dst_row)], sem)


def _wait_rows(buf, nrows, sem):
    view = buf.at[pl.ds(0, nrows * ROW_TILE)]
    pltpu.make_async_copy(view, view, sem).wait()


def _load_rows(ref, nrows):
    words = [ref[pl.ds(j, nrows, stride=ROW_TILE), :] for j in range(ROW_TILE)]
    low = [lax.bitcast_convert_type(lax.shift_left(w, U32(16)), F32) for w in words]
    high = [lax.bitcast_convert_type(w & U32(0xFFFF0000), F32) for w in words]
    return jnp.concatenate(low + high, axis=-1)


def _store_rows(ref, val):
    bits = lambda x: lax.bitcast_convert_type(x.astype(BF16).astype(F32), U32)
    half = D_MODEL // 2
    for j in range(ROW_TILE):
        low, high = val[:, j * LANES:(j + 1) * LANES], val[:, half + j * LANES:half + (j + 1) * LANES]
        ref[pl.ds(j, val.shape[0], stride=ROW_TILE), :] = lax.shift_right_logical(bits(low), U32(16)) | bits(high)


def _dispatch_kernel(zb_ref, dest_ref, hn_ref, xs_out, zbuf, sem, zsem):
    first = pl.program_id(1) == 0

    @pl.when((pl.program_id(0) == 0) & first)
    def _():
        zbuf[...] = jnp.zeros_like(zbuf)

        def block_copy(j):
            return pltpu.make_async_copy(zbuf, xs_out.at[_tile_rows(jnp.maximum(zb_ref[j], 0) * MOE_BLOCK, MOE_BLOCK)], zsem)

        def start(j, carry):
            @pl.when(zb_ref[j] >= 0)
            def _():
                block_copy(j).start()
            return carry

        def finish(j, carry):
            @pl.when(zb_ref[j] >= 0)
            def _():
                block_copy(j).wait()
            return carry

        lax.fori_loop(0, ZERO_BLOCKS, start, 0)
        lax.fori_loop(0, ZERO_BLOCKS, finish, 0)

    def issue(blk, lo):
        def body(r, carry):
            for k in range(2):
                _row_copy(hn_ref, blk * CHUNK + r, xs_out, dest_ref[blk, k, r], sem).start(priority=k)
            return carry
        lax.fori_loop(lo, CHUNK, body, 0, unroll=DMA_UNROLL)

    @pl.when(first)
    def _():
        issue(0, LEAD)

    @pl.when(jnp.logical_not(first))
    def _():
        issue(0, 0)

    for blk in range(1, CPT):
        issue(blk, 0)

    @pl.when(first)
    def _():
        for _k in range(2):
            _wait_rows(hn_ref, TM - LEAD, sem)

    @pl.when(jnp.logical_not(first))
    def _():
        for _k in range(2):
            _wait_rows(hn_ref, TM, sem)


def _dispatch(zero_blocks, dest, hn, nrows, bsz, lp):
    tiles = lp // TM
    return pl.pallas_call(
        _dispatch_kernel,
        grid_spec=pltpu.PrefetchScalarGridSpec(
            num_scalar_prefetch=1,
            grid=(bsz, tiles),
            in_specs=[pl.BlockSpec((CPT, 2, CHUNK), lambda b, c, zb: (b * tiles + c, 0, 0),
                                   memory_space=pltpu.SMEM),
                      pl.BlockSpec((TM * ROW_TILE, LANES), lambda b, c, zb: (b * tiles + c, 0))],
            out_specs=pl.BlockSpec(memory_space=pl.ANY),
            scratch_shapes=[pltpu.VMEM((MOE_BLOCK * ROW_TILE, LANES), U32),
                            pltpu.SemaphoreType.DMA(()),
                            pltpu.SemaphoreType.DMA(())]),
        out_shape=jax.ShapeDtypeStruct((nrows * ROW_TILE, LANES), U32),
        compiler_params=_cparams("arbitrary", "arbitrary"),
        name="moe_dispatch",
    )(zero_blocks, dest, hn)


def _expert_kernel(blk0_ref, nblk_ref, nused_ref, x_hbm, wgu_ref, wdn_ref, y_hbm,
                   xbuf, ybuf, wgu16, wdn16, xsem, ysem):
    e = pl.program_id(0)
    nblk = nblk_ref[e]
    nused = nused_ref[0]
    nb = y_hbm.shape[0] // (MOE_BLOCK * ROW_TILE)

    def x_copy(g, slot):
        return pltpu.make_async_copy(x_hbm.at[_tile_rows(g * MOE_BLOCK, MOE_BLOCK)], xbuf.at[slot], xsem.at[slot])

    def y_copy(g, slot):
        return pltpu.make_async_copy(ybuf.at[slot], y_hbm.at[_tile_rows(g * MOE_BLOCK, MOE_BLOCK)], ysem.at[slot])

    @pl.when((e == 0) & (nused > 0))
    def _():
        x_copy(0, 0).start()

    @pl.when(nblk > 0)
    def _():
        wgu16[...] = wgu_ref[0].astype(BF16)
        wdn16[...] = wdn_ref[0].astype(BF16)

        def body(j, carry):
            g = blk0_ref[e] + j
            slot = lax.rem(g, 2)

            @pl.when(g + 1 < nused)
            def _():
                x_copy(g + 1, 1 - slot).start()

            x_copy(g, slot).wait()

            @pl.when(g >= 2)
            def _():
                y_copy(g - 2, slot).wait()

            gu = jnp.dot(_load_rows(xbuf.at[slot], MOE_BLOCK).astype(BF16), wgu16[...],
                         preferred_element_type=F32)
            gate, up = gu[:, :D_EXPERT], gu[:, D_EXPERT:]
            act = gate * _sigmoid(gate) * up
            _store_rows(ybuf.at[slot], jnp.dot(act.astype(BF16), wdn16[...], preferred_element_type=F32))
            y_copy(g, slot).start()
            return carry

        lax.fori_loop(0, nblk, body, 0)

    @pl.when(e == pl.num_programs(0) - 1)
    def _():
        @pl.when(nused >= 2)
        def _():
            y_copy(nused - 2, lax.rem(nused, 2)).wait()

        @pl.when(nused >= 1)
        def _():
            y_copy(nused - 1, lax.rem(nused + 1, 2)).wait()

        ybuf[0] = jnp.zeros((MOE_BLOCK * ROW_TILE, LANES), U32)

        def zero_start(g, carry):
            y_copy(g, 0).start()
            return carry

        def zero_wait(g, carry):
            y_copy(g, 0).wait()
            return carry

        lax.fori_loop(nused, nb, zero_start, 0)
        lax.fori_loop(nused, nb, zero_wait, 0)


def _experts(blk0, nblk, nused, xs, w_gu, w_dn):
    wsel = lambda e, b0, nbk, nu: (e, 0, 0)
    return pl.pallas_call(
        _expert_kernel,
        grid_spec=pltpu.PrefetchScalarGridSpec(
            num_scalar_prefetch=3,
            grid=(N_EXPERTS,),
            in_specs=[pl.BlockSpec(memory_space=pl.ANY),
                      pl.BlockSpec((1, D_MODEL, 2 * D_EXPERT), wsel),
                      pl.BlockSpec((1, D_EXPERT, D_MODEL), wsel)],
            out_specs=pl.BlockSpec(memory_space=pl.ANY),
            scratch_shapes=[pltpu.VMEM((2, MOE_BLOCK * ROW_TILE, LANES), U32),
                            pltpu.VMEM((2, MOE_BLOCK * ROW_TILE, LANES), U32),
                            pltpu.VMEM((D_MODEL, 2 * D_EXPERT), BF16),
                            pltpu.VMEM((D_EXPERT, D_MODEL), BF16),
                            pltpu.SemaphoreType.DMA((2,)),
                            pltpu.SemaphoreType.DMA((2,))]),
        out_shape=jax.ShapeDtypeStruct(xs.shape, U32),
        compiler_params=_cparams("arbitrary"),
        name="moe_experts",
    )(blk0, nblk, nused, xs, w_gu, w_dn)


def _combine_kernel(da_ref, db_ref, dn_ref, h1a_ref, h1b_ref, rta_ref, rtb_ref, nf_ref, y_hbm,
                    o_ref, ya, yb, sem):
    step = pl.program_id(0) * pl.num_programs(1) + pl.program_id(1)
    nsteps = pl.num_programs(0) * pl.num_programs(1)

    def issue(d_ref, slot):
        def body(r, carry):
            _row_copy(y_hbm, d_ref[0, 0, r], ya.at[slot], r, sem.at[slot, 0]).start(priority=0)
            _row_copy(y_hbm, d_ref[0, 1, r], yb.at[slot], r, sem.at[slot, 1]).start(priority=1)
            return carry
        lax.fori_loop(0, CHUNK, body, 0, unroll=DMA_UNROLL)

    def finish(slot, h1_ref, rt_ref):
        _wait_rows(ya.at[slot], CHUNK, sem.at[slot, 0])
        _wait_rows(yb.at[slot], CHUNK, sem.at[slot, 1])
        rt = rt_ref[...]
        h2 = (h1_ref[...] + rt[:, 0:1] * _load_rows(ya.at[slot], CHUNK)
              + rt[:, 1:2] * _load_rows(yb.at[slot], CHUNK))
        o_ref[0, slot * CHUNK:(slot + 1) * CHUNK, :] = (
            h2 * lax.rsqrt(jnp.mean(h2 * h2, axis=-1, keepdims=True) + EPS) * nf_ref[...])

    @pl.when(step == 0)
    def _():
        issue(da_ref, 0)

    issue(db_ref, 1)
    finish(0, h1a_ref, rta_ref)

    @pl.when(step + 1 < nsteps)
    def _():
        issue(dn_ref, 0)

    finish(1, h1b_ref, rtb_ref)


def _combine(dest, h1, wts, norm_final, ys, bsz, lp, seq):
    nc = lp // CHUNK
    half = seq // (2 * CHUNK)
    first = lambda b, j: b * nc + 1 + 2 * j
    second = lambda b, j: b * nc + 2 + 2 * j

    def upcoming(b, j):
        nxt = jnp.minimum(b * half + j + 1, bsz * half - 1)
        return first(nxt // half, nxt % half)

    smem3 = lambda f: pl.BlockSpec((1, 2, CHUNK), lambda b, j: (f(b, j), 0, 0), memory_space=pltpu.SMEM)
    rows2 = lambda f, width: pl.BlockSpec((CHUNK, width), lambda b, j: (f(b, j), 0))
    return pl.pallas_call(
        _combine_kernel,
        grid=(bsz, half),
        in_specs=[smem3(first), smem3(second), smem3(upcoming),
                  rows2(first, D_MODEL), rows2(second, D_MODEL),
                  rows2(first, 2), rows2(second, 2),
                  pl.BlockSpec((1, D_MODEL), lambda b, j: (0, 0)),
                  pl.BlockSpec(memory_space=pl.ANY)],
        out_specs=pl.BlockSpec((1, 2 * CHUNK, D_MODEL), lambda b, j: (b, j, 0)),
        out_shape=jax.ShapeDtypeStruct((bsz, seq, D_MODEL), F32),
        scratch_shapes=[pltpu.VMEM((2, CHUNK * ROW_TILE, LANES), U32),
                        pltpu.VMEM((2, CHUNK * ROW_TILE, LANES), U32),
                        pltpu.SemaphoreType.DMA((2, 2))],
        compiler_params=_cparams("arbitrary", "arbitrary"),
        name="moe_combine",
    )(dest, dest, dest, h1, h1, wts, wts, norm_final, ys)


def _row(v):
    return v.reshape(1, -1).astype(F32)


def _pad_lanes(v):
    return jnp.pad(v, ((0, 0), (0, LANES - v.shape[1])))


def kernel(x, meta_tokens, norm_mix, w_in, pool_w, pool_scale, conv_w, conv_b, dt_bias, a_log, d_skip,
           ssd_norm, w_pool_out, w_ssd_out, w_out, norm_ffn, w_router_group, b_router_group,
           w_router_expert, b_router_expert, w_gate_up, w_down, norm_final):
    bsz, seq, _ = x.shape
    lp = LEAD + N_META + seq
    assert lp % TM == 0 and seq % (2 * CHUNK) == 0
    rows = bsz * lp

    meta = meta_tokens.astype(x.dtype)

    wi = w_in[0]
    o_z, o_x, o_dt, o_gp, o_gs = 1024, 3072, 6144, 6176, 7200
    cw = conv_w[0].astype(F32)
    cb = _row(conv_b[0])
    grp = lambda a, off, g, n: a[:, off + g * n:off + (g + 1) * n]
    w_groups, cw_groups, cb_groups = [], [], []
    for g in range(SSD_GROUPS):
        conv_cols = lambda a: jnp.concatenate(
            [grp(a, 0, g, GROUP_X), grp(a, SSD_INNER, g, SSD_STATE),
             grp(a, SSD_INNER + SSD_GROUPS * SSD_STATE, g, SSD_STATE)], axis=1)
        half_cols = jnp.concatenate([grp(wi, o_z, g, GROUP_Z), grp(wi, o_gp, g, GROUP_GATE // 2),
                                     grp(wi, o_gs, g, GROUP_GATE // 2)], axis=1)
        w_groups.append(jnp.concatenate([conv_cols(wi[:, o_x:o_dt]), 0.5 * half_cols,
                                         grp(wi, 0, g, GROUP_U)], axis=1).astype(BF16))
        cw_groups.append(0.5 * conv_cols(cw))
        cb_groups.append(0.5 * conv_cols(cb))
    xbc_act, z_act, gates, u, dt_raw = _projections(
        x, meta, _row(norm_mix[0]), jnp.stack(w_groups), jnp.stack(cw_groups), jnp.stack(cb_groups),
        _pad_lanes(wi[:, o_dt:o_gp]).astype(BF16), lp)

    ys = _ssd_branch(z_act, xbc_act, dt_raw, gates,
                     _pad_lanes(_row(dt_bias[0])), _pad_lanes(_row(a_log[0])),
                     _row(jnp.repeat(d_skip[0], SSD_HEAD_DIM)), _row(ssd_norm[0]),
                     w_ssd_out[0].astype(BF16), bsz, lp)

    w_router = jnp.concatenate([w_router_group[0], w_router_expert[0]], axis=1).astype(F32).T
    w_router = jnp.pad(w_router, ((0, LANES - w_router.shape[0]), (0, 0)))
    wr_hi = w_router.astype(BF16)
    wr_lo = (w_router - wr_hi.astype(F32)).astype(BF16)
    b_router = _pad_lanes(jnp.concatenate([_row(b_router_group[0]), _row(b_router_expert[0])], axis=1))
    b_router = jnp.broadcast_to(b_router.reshape(LANES, 1), (LANES, TM))
    h1, hn, route, counts = _merge_router(x, meta, u, gates, ys, pool_w[0].astype(BF16), _row(pool_scale[0]),
                                          w_pool_out[0].astype(BF16), w_out[0].astype(BF16),
                                          _row(norm_ffn[0]), wr_hi, wr_lo, b_router, lp)

    n_assign = 2 * bsz * (N_META + seq)
    nb = -(-n_assign // MOE_BLOCK) + N_EXPERTS
    cnt = counts[:N_EXPERTS, 0].astype(I32)
    pcnt = (cnt + MOE_BLOCK - 1) // MOE_BLOCK * MOE_BLOCK
    pends = jnp.cumsum(pcnt)
    pstarts = pends - pcnt
    experts = route[0:2].astype(I32)
    eids = jnp.arange(N_EXPERTS, dtype=I32)[:, None]
    start_of = jnp.sum(jnp.where(experts[:, None, :] == eids, pstarts[:, None], 0), axis=1)
    dest = start_of + route[2:4].astype(I32)
    dest = dest.reshape(2, rows // CHUNK, CHUNK).transpose(1, 0, 2)
    wts = route[4:6].T
    nused = (pends[-1:] // MOE_BLOCK).astype(I32)
    last_blocks = jnp.where(pcnt > 0, pends // MOE_BLOCK - 1, -1)
    tail_ids = nused[0] + jnp.arange(ZERO_BLOCKS - N_EXPERTS, dtype=I32)
    zero_blocks = jnp.concatenate([last_blocks, jnp.where(tail_ids < nb, tail_ids, -1)]).astype(I32)

    xs = _dispatch(zero_blocks, dest, hn, nb * MOE_BLOCK, bsz, lp)
    yexp = _experts((pstarts // MOE_BLOCK).astype(I32), (pcnt // MOE_BLOCK).astype(I32), nused,
                    xs, w_gate_up[0], w_down[0])
    return _combine(dest, h1, wts, _row(norm_final), yexp, bsz, lp, seq)
```
